```python
import math
import jax, jax.numpy as jnp
from jax import lax
import numpy as np

D_MODEL = 2048
BATCH = 16
SEQ = 2048
DEPTH = 4

GRID_W = 64
CTX_LEN = 256
N_BRANCH = 4
MIX_WIDTH = D_MODEL // 2
N_MOD = 9
HALF_STEP = 0.5
FFN_DIM = 5632
SSD_HEADDIM = 64
SSD_HEADS = MIX_WIDTH // SSD_HEADDIM
SSD_GROUPS = 4
SSD_HEADS_PER_GROUP = SSD_HEADS // SSD_GROUPS
SSD_STATE = 128
SSD_CONV = 5
SSD_CHUNK = 128
SSD_CONV_DIM = MIX_WIDTH + 2 * SSD_GROUPS * SSD_STATE
GMLP_GROUPS = 8
GMLP_CHUNK = 128
CONV_KERNEL = 31
HEAD_DIM = 128
ATTN_HEADS = MIX_WIDTH // HEAD_DIM
ATTN_KV_HEADS = ATTN_HEADS // 4
WINDOW = 128
ATTN_BLOCK = WINDOW
ROPE_BASE = 10000.0
EPS = 1e-6
IN_SIZES = (MIX_WIDTH, SSD_CONV_DIM, SSD_HEADS, 2 * MIX_WIDTH, 2 * MIX_WIDTH,
            ATTN_HEADS * HEAD_DIM, ATTN_KV_HEADS * HEAD_DIM, ATTN_KV_HEADS * HEAD_DIM)
IN_WIDTH = MIX_WIDTH + SSD_CONV_DIM + SSD_HEADS + 4 * MIX_WIDTH + (ATTN_HEADS + 2 * ATTN_KV_HEADS) * HEAD_DIM

kernel_name = 'hybrid_parallel_mixer_dit'


def rms_norm(x, g):
    xf = x.astype(jnp.float32)
    y = xf * lax.rsqrt(jnp.mean(xf * xf, axis=-1, keepdims=True) + EPS)
    return (y * g.astype(jnp.float32)).astype(x.dtype)


def layer_norm(x, g, b):
    xf = x.astype(jnp.float32)
    mu = jnp.mean(xf, axis=-1, keepdims=True)
    var = jnp.mean(jnp.square(xf - mu), axis=-1, keepdims=True)
    y = (xf - mu) * lax.rsqrt(var + EPS) * g.astype(jnp.float32) + b.astype(jnp.float32)
    return y.astype(x.dtype)


def modulated_norm(x, g, m, k):
    shift, scale = m[..., 3 * k, :, :], m[..., 3 * k + 1, :, :]
    return rms_norm(x, g) * (1 + scale) + shift


def sub_gate(m, k):
    return m[..., 3 * k + 2, :, :]


def swiglu(h, w13, w2):
    a, b = jnp.split(h @ w13, 2, axis=-1)
    return (jax.nn.silu(a) * b) @ w2


def ffn_half_step(x, m, k, g, w13, w2):
    h = modulated_norm(x, g, m, k)
    return x + HALF_STEP * sub_gate(m, k) * swiglu(h, w13, w2)


def dwconv_centred(x, w, b):
    K = w.shape[0]
    pad = (K - 1) // 2
    y = lax.conv_general_dilated(x, w[:, None, :], window_strides=(1,), padding=[(pad, pad)],
                                 dimension_numbers=('NWC', 'WIO', 'NWC'), feature_group_count=x.shape[-1])
    return y + b


def ssd_chunk_scan(xh, dt, A, Bm, Cm, Dskip, h0):
    Bsz, T, G, R, P = xh.shape
    N = Bm.shape[-1]
    Q = SSD_CHUNK
    nc = T // Q
    f32 = jnp.float32
    x = xh.astype(f32).reshape(Bsz, nc, Q, G, R, P)
    dtc = dt.reshape(Bsz, nc, Q, G, R)
    Bc = Bm.astype(f32).reshape(Bsz, nc, Q, G, N)
    Cc = Cm.astype(f32).reshape(Bsz, nc, Q, G, N)
    acs = jnp.cumsum(dtc * A, axis=2)
    tri = jnp.tril(jnp.ones((Q, Q), bool))
    seg = acs[:, :, :, None] - acs[:, :, None, :]
    decay = jnp.exp(jnp.where(tri[:, :, None, None], seg, -jnp.inf))
    xdt = x * dtc[..., None]
    cb = jnp.einsum('bcign,bcjgn->bcijg', Cc, Bc)
    y_diag = jnp.einsum('bcijgr,bcjgrp->bcigrp', cb[..., None] * decay, xdt)
    to_end = jnp.exp(acs[:, :, -1:] - acs)
    states = jnp.einsum('bcjgn,bcjgrp->bcgrpn', Bc, xdt * to_end[..., None])
    chunk_decay = jnp.exp(acs[:, :, -1])

    def step(h, inp):
        s, dcy = inp
        return h * dcy[..., None, None] + s, h

    h_final, h_enter = lax.scan(step, h0, (jnp.moveaxis(states, 1, 0), jnp.moveaxis(chunk_decay, 1, 0)))
    h_enter = jnp.moveaxis(h_enter, 0, 1)
    y_off = jnp.einsum('bcign,bcgrpn->bcigrp', Cc, h_enter) * jnp.exp(acs)[..., None]
    y = y_diag + y_off + x * Dskip[..., None]
    return y.reshape(Bsz, T, G, R, P), h_final


def _ssd_prepare(xbc, conv_w, conv_b):
    Bsz, T, _ = xbc.shape
    h = jax.nn.silu(dwconv_centred(xbc, conv_w, conv_b))
    xh, Bm, Cm = jnp.split(h, [MIX_WIDTH, MIX_WIDTH + SSD_GROUPS * SSD_STATE], axis=-1)
    xh = xh.reshape(Bsz, T, SSD_GROUPS, SSD_HEADS_PER_GROUP, SSD_HEADDIM)
    Bm = Bm.reshape(Bsz, T, SSD_GROUPS, SSD_STATE)
    Cm = Cm.reshape(Bsz, T, SSD_GROUPS, SSD_STATE)
    return xh, Bm, Cm


def _orient(t, d):
    return t if d == 0 else jnp.flip(t, axis=1)


def ssd_branch(z_c, xbc_c, dt_c, z_s, xbc_s, dt_s, conv_w, conv_b, dt_bias, a_log, d_skip, norm_g):
    G, R = SSD_GROUPS, SSD_HEADS_PER_GROUP
    xc, Bc, Cc = _ssd_prepare(xbc_c, conv_w, conv_b)
    xs, Bs, Cs = _ssd_prepare(xbc_s, conv_w, conv_b)
    Bsz = xs.shape[0]
    h0 = jnp.zeros((Bsz, G, R, SSD_HEADDIM, SSD_STATE), jnp.float32)
    yc_dirs, ys_dirs = [], []
    for d in range(2):
        A = -jnp.exp(a_log[d].astype(jnp.float32)).reshape(G, R)
        Dd = d_skip[d].astype(jnp.float32).reshape(G, R)
        bias = dt_bias[d].astype(jnp.float32)
        dtc = jax.nn.softplus(dt_c.astype(jnp.float32) + bias).reshape(dt_c.shape[:2] + (G, R))
        dts = jax.nn.softplus(dt_s.astype(jnp.float32) + bias).reshape(dt_s.shape[:2] + (G, R))
        yc_d, hc = ssd_chunk_scan(_orient(xc, d), _orient(dtc, d), A, _orient(Bc, d), _orient(Cc, d), Dd, h0)
        ys_d, _ = ssd_chunk_scan(_orient(xs, d), _orient(dts, d), A, _orient(Bs, d), _orient(Cs, d), Dd, hc)
        yc_dirs.append(_orient(yc_d, d))
        ys_dirs.append(_orient(ys_d, d))

    def out(y, z):
        y = y.reshape(y.shape[:2] + (MIX_WIDTH,)).astype(z.dtype)
        return rms_norm(y * jax.nn.silu(z), norm_g)

    return out(yc_dirs[0] + yc_dirs[1], z_c), out(ys_dirs[0] + ys_dirs[1], z_s)


def chunk_gmlp(p, norm_g, ws, bs):
    u, v = jnp.split(jax.nn.gelu(p), 2, axis=-1)
    v = rms_norm(v, norm_g)
    Bsz, T, W = v.shape
    vc = v.reshape(Bsz, T // GMLP_CHUNK, GMLP_CHUNK, GMLP_GROUPS, W // GMLP_GROUPS)
    mixed = jnp.einsum('gij,bcjgd->bcigd', ws, vc) + bs.T[:, :, None]
    return u * mixed.reshape(Bsz, T, W)


def conformer_conv(p, dw_w, dw_b, ln_g, ln_b):
    a, g = jnp.split(p, 2, axis=-1)
    h = dwconv_centred(a * jax.nn.sigmoid(g), dw_w, dw_b)
    return jax.nn.silu(layer_norm(h, ln_g, ln_b))


def axial_rope(x, rows, cols):
    Dh = x.shape[-1]
    quarter = Dh // 4
    freqs = ROPE_BASE ** (-jnp.arange(quarter, dtype=jnp.float32) / quarter)
    ang = jnp.stack([rows.astype(jnp.float32)[:, None] * freqs,
                     cols.astype(jnp.float32)[:, None] * freqs], axis=1)
    shp = (ang.shape[0],) + (1,) * (x.ndim - 3) + (2, quarter)
    cos, sin = jnp.cos(ang).reshape(shp), jnp.sin(ang).reshape(shp)
    xr = x.astype(jnp.float32).reshape(x.shape[:-1] + (2, 2, quarter))
    x1, x2 = xr[..., 0, :], xr[..., 1, :]
    out = jnp.stack([x1 * cos - x2 * sin, x2 * cos + x1 * sin], axis=-2)
    return out.reshape(x.shape).astype(x.dtype)


def attention_branch(q_c, k_c, v_c, q_s, k_s, v_s, sink, rows, cols):
    Bsz, S, _ = q_s.shape
    L = q_c.shape[1]
    Hk, R, Dh = ATTN_KV_HEADS, ATTN_HEADS // ATTN_KV_HEADS, HEAD_DIM
    scale = Dh ** -0.5
    q_s = axial_rope(q_s.reshape(Bsz, S, Hk, R, Dh), rows, cols)
    k_s = axial_rope(k_s.reshape(Bsz, S, Hk, Dh), rows, cols)
    v_s = v_s.reshape(Bsz, S, Hk, Dh)
    q_c = q_c.reshape(Bsz, L, Hk, R, Dh)
    k_c = k_c.reshape(Bsz, L, Hk, Dh)
    v_c = v_c.reshape(Bsz, L, Hk, Dh)
    sink = sink.astype(jnp.float32).reshape(Hk, R)
    s_cc = jnp.einsum('bqgrd,bkgd->bgrqk', q_c, k_c).astype(jnp.float32) * scale
    sk_c = jnp.broadcast_to(sink[None, :, :, None, None], s_cc.shape[:-1] + (1,))
    p_cc = jax.nn.softmax(jnp.concatenate([s_cc, sk_c], axis=-1), axis=-1)[..., :L]
    o_c = jnp.einsum('bgrqk,bkgd->bqgrd', p_cc.astype(v_c.dtype), v_c).reshape(Bsz, L, Hk * R * Dh)
    nb = S // ATTN_BLOCK
    nk = 3 * ATTN_BLOCK
    pad = ((0, 0), (WINDOW, WINDOW), (0, 0), (0, 0))
    kp = jnp.pad(k_s, pad).reshape(Bsz, nb + 2, ATTN_BLOCK, Hk, Dh)
    vp = jnp.pad(v_s, pad).reshape(Bsz, nb + 2, ATTN_BLOCK, Hk, Dh)
    kb = jnp.concatenate([kp[:, :-2], kp[:, 1:-1], kp[:, 2:]], axis=2)
    vb = jnp.concatenate([vp[:, :-2], vp[:, 1:-1], vp[:, 2:]], axis=2)
    qb = q_s.reshape(Bsz, nb, ATTN_BLOCK, Hk, R, Dh)
    s_band = jnp.einsum('bnqgrd,bnkgd->bngrqk', qb, kb).astype(jnp.float32) * scale
    s_ctx = jnp.einsum('bnqgrd,bkgd->bngrqk', qb, k_c).astype(jnp.float32) * scale
    kpos = (jnp.arange(nb) * ATTN_BLOCK - WINDOW)[:, None] + jnp.arange(nk)[None, :]
    qpos = (jnp.arange(nb) * ATTN_BLOCK)[:, None] + jnp.arange(ATTN_BLOCK)[None, :]
    mask = (jnp.abs(kpos[:, None, :] - qpos[:, :, None]) <= WINDOW) & ((kpos >= 0) & (kpos < S))[:, None, :]
    s_band = jnp.where(mask[None, :, None, None], s_band, -jnp.inf)
    sk_s = jnp.broadcast_to(sink[None, None, :, :, None, None], s_band.shape[:-1] + (1,))
    p = jax.nn.softmax(jnp.concatenate([s_band, s_ctx, sk_s], axis=-1), axis=-1)
    o_s = (jnp.einsum('bngrqk,bnkgd->bnqgrd', p[..., :nk].astype(vb.dtype), vb)
           + jnp.einsum('bngrqk,bkgd->bnqgrd', p[..., nk:nk + L].astype(v_c.dtype), v_c))
    return o_c, o_s.reshape(Bsz, S, Hk * R * Dh)


def merge_branches(h, ys, w_gate, b_gate, w_branch, w_out):
    merged = None
    for b in range(N_BRANCH):
        term = jax.nn.sigmoid(h @ w_gate[b] + b_gate[b]) * (ys[b] @ w_branch[b])
        merged = term if merged is None else merged + term
    return merged @ w_out


def setup_inputs(seed: int = 0) -> dict:
    key = jax.random.key(seed)
    keys = iter(jax.random.split(key, 48))
    f32 = jnp.float32
    D, L = D_MODEL, DEPTH

    def nrm(shape, scale):
        return jax.random.normal(next(keys), shape, f32) * scale

    def gain(shape):
        return 1.0 + 0.1 * jax.random.normal(next(keys), shape, f32)

    dt0 = jnp.exp(jax.random.uniform(next(keys), (L, 2, SSD_HEADS), f32, math.log(1e-3), math.log(1e-1)))
    dt_bias = dt0 + jnp.log(-jnp.expm1(-dt0))
    a_log = jnp.log(jax.random.uniform(next(keys), (L, 2, SSD_HEADS), f32, 1.0, 16.0))
    return {
        'x': nrm((BATCH, SEQ, D), 1.0),
        'c': nrm((BATCH, D), 1.0),
        'ctx': nrm((BATCH, CTX_LEN, D), 1.0),
        'c_ctx': nrm((D,), 1.0),
        'w_ada': nrm((L, D, N_MOD * D), 0.3 * D ** -0.5),
        'b_ada': nrm((L, N_MOD * D), 0.02),
        'ffn1_norm': gain((L, D)),
        'ffn1_w13': nrm((L, D, 2 * FFN_DIM), D ** -0.5),
        'ffn1_w2': nrm((L, FFN_DIM, D), FFN_DIM ** -0.5),
        'mix_norm': gain((L, D)),
        'w_in': nrm((L, D, IN_WIDTH), D ** -0.5),
        'w_gate': nrm((L, N_BRANCH, D, D), D ** -0.5),
        'b_gate': nrm((L, N_BRANCH, D), 0.1),
        'w_branch': nrm((L, N_BRANCH, MIX_WIDTH, D), MIX_WIDTH ** -0.5),
        'w_out': nrm((L, D, D), D ** -0.5),
        'ssd_conv_w': nrm((L, SSD_CONV, SSD_CONV_DIM), SSD_CONV ** -0.5),
        'ssd_conv_b': nrm((L, SSD_CONV_DIM), 0.02),
        'ssd_dt_bias': dt_bias,
        'ssd_a_log': a_log,
        'ssd_d': gain((L, 2, SSD_HEADS)),
        'ssd_norm': gain((L, MIX_WIDTH)),
        'gmlp_norm': gain((L, MIX_WIDTH)),
        'gmlp_ws': nrm((L, GMLP_GROUPS, GMLP_CHUNK, GMLP_CHUNK), GMLP_CHUNK ** -0.5),
        'gmlp_bs': gain((L, GMLP_GROUPS, GMLP_CHUNK)),
        'conv_dw_w': nrm((L, CONV_KERNEL, MIX_WIDTH), CONV_KERNEL ** -0.5),
        'conv_dw_b': nrm((L, MIX_WIDTH), 0.02),
        'conv_ln_g': gain((L, MIX_WIDTH)),
        'conv_ln_b': nrm((L, MIX_WIDTH), 0.02),
        'attn_sink': nrm((L, ATTN_HEADS), 0.5),
        'ffn2_norm': gain((L, D)),
        'ffn2_w13': nrm((L, D, 2 * FFN_DIM), D ** -0.5),
        'ffn2_w2': nrm((L, FFN_DIM, D), FFN_DIM ** -0.5),
        'final_norm': gain((D,)),
    }


def reference(x, c, ctx, c_ctx, w_ada, b_ada, ffn1_norm, ffn1_w13, ffn1_w2, mix_norm, w_in, w_gate,
              b_gate, w_branch, w_out, ssd_conv_w, ssd_conv_b, ssd_dt_bias, ssd_a_log, ssd_d, ssd_norm,
              gmlp_norm, gmlp_ws, gmlp_bs, conv_dw_w, conv_dw_b, conv_ln_g, conv_ln_b, attn_sink,
              ffn2_norm, ffn2_w13, ffn2_w2, final_norm):
    Bsz, S, D = x.shape
    ROWS = S // GRID_W
    rows = jnp.repeat(jnp.arange(ROWS, dtype=jnp.int32), GRID_W)
    cols = jnp.tile(jnp.arange(GRID_W, dtype=jnp.int32), ROWS)
    split_at = [int(i) for i in np.cumsum(IN_SIZES)[:-1]]
    xs, xc = x, ctx
    for l in range(DEPTH):
        last = l == DEPTH - 1
        m_s = (jax.nn.silu(c) @ w_ada[l] + b_ada[l]).reshape(Bsz, N_MOD, 1, D)
        m_c = (jax.nn.silu(c_ctx) @ w_ada[l] + b_ada[l]).reshape(N_MOD, 1, D)
        xs = ffn_half_step(xs, m_s, 0, ffn1_norm[l], ffn1_w13[l], ffn1_w2[l])
        xc = ffn_half_step(xc, m_c, 0, ffn1_norm[l], ffn1_w13[l], ffn1_w2[l])
        h_s = modulated_norm(xs, mix_norm[l], m_s, 1)
        h_c = modulated_norm(xc, mix_norm[l], m_c, 1)
        z_c, xbc_c, dt_c, gm_c, cv_c, q_c, k_c, v_c = jnp.split(h_c @ w_in[l], split_at, axis=-1)
        z_s, xbc_s, dt_s, gm_s, cv_s, q_s, k_s, v_s = jnp.split(h_s @ w_in[l], split_at, axis=-1)
        ya_c, ya_s = ssd_branch(z_c, xbc_c, dt_c, z_s, xbc_s, dt_s, ssd_conv_w[l], ssd_conv_b[l],
                                ssd_dt_bias[l], ssd_a_log[l], ssd_d[l], ssd_norm[l])
        yd_c, yd_s = attention_branch(q_c, k_c, v_c, q_s, k_s, v_s, attn_sink[l], rows, cols)
        ys = (ya_s, chunk_gmlp(gm_s, gmlp_norm[l], gmlp_ws[l], gmlp_bs[l]),
              conformer_conv(cv_s, conv_dw_w[l], conv_dw_b[l], conv_ln_g[l], conv_ln_b[l]), yd_s)
        xs = xs + sub_gate(m_s, 1) * merge_branches(h_s, ys, w_gate[l], b_gate[l], w_branch[l], w_out[l])
        if not last:
            yc = (ya_c, chunk_gmlp(gm_c, gmlp_norm[l], gmlp_ws[l], gmlp_bs[l]),
                  conformer_conv(cv_c, conv_dw_w[l], conv_dw_b[l], conv_ln_g[l], conv_ln_b[l]), yd_c)
            xc = xc + sub_gate(m_c, 1) * merge_branches(h_c, yc, w_gate[l], b_gate[l], w_branch[l], w_out[l])
        xs = ffn_half_step(xs, m_s, 2, ffn2_norm[l], ffn2_w13[l], ffn2_w2[l])
        if not last:
            xc = ffn_half_step(xc, m_c, 2, ffn2_norm[l], ffn2_w13[l], ffn2_w2[l])
    return rms_norm(xs, final_norm)
```

```python
import functools
import math

import jax
import jax.numpy as jnp
from jax import lax
from jax.experimental import pallas as pl
from jax.experimental.pallas import tpu as pltpu

F32 = jnp.float32
BF16 = jnp.bfloat16

GRID_W = 64
N_BRANCH = 4
N_MOD = 9
HALF_STEP = 0.5
SSD_HEADDIM = 64
SSD_GROUPS = 4
SSD_STATE = 128
SSD_CONV = 5
SSD_CHUNK = 128
GMLP_GROUPS = 8
GMLP_CHUNK = 128
CONV_KERNEL = 31
HEAD_DIM = 128
KV_RATIO = 4
WINDOW = 128
ROPE_BASE = 10000.0
EPS = 1e-6

LANES = 128
SUBLANES = 8
VMEM_LIMIT_BYTES = 56 * 1024 * 1024

ROW_BLOCK = 256
SSD_HALO = 8
CONF_HALO = 16
MOD_ROWS = 24


def _params(*sem):
    return pltpu.CompilerParams(dimension_semantics=sem, vmem_limit_bytes=VMEM_LIMIT_BYTES)


def _silu(x):
    return x * jax.nn.sigmoid(x)


def _dot(a, b):
    return jnp.dot(a, b, preferred_element_type=F32)


def _dot_nt(a, b):
    return lax.dot_general(a, b, (((1,), (1,)), ((), ())), preferred_element_type=F32)


def _rms(x, g):
    return x * lax.rsqrt(jnp.mean(x * x, axis=-1, keepdims=True) + EPS) * g


def _split3(v):
    hi = v.astype(BF16)
    r1 = v - hi.astype(F32)
    mid = r1.astype(BF16)
    lo = (r1 - mid.astype(F32)).astype(BF16)
    return hi, mid, lo


def _exact_dot_right(v, m01):
    hi, mid, lo = _split3(v)
    return _dot(hi, m01) + _dot(mid, m01) + _dot(lo, m01)


def _exact_dot_left(m01, v):
    hi, mid, lo = _split3(v)
    return _dot(m01, hi) + _dot(m01, mid) + _dot(m01, lo)


def _row_select(is_ctx, ctx_ref, batch_ref):
    return jnp.where(is_ctx, ctx_ref[...], batch_ref[...])


def _ctx_rows(tile, tiles_per_batch, ctx_len, tm):
    rows = lax.broadcasted_iota(jnp.int32, (tm, 1), 0)
    return jnp.logical_and(tile % tiles_per_batch == 0, rows < ctx_len)


def _ada_kernel(c_ref, w_ref, b_ref, o_ref):
    a = _silu(c_ref[...]).astype(BF16)
    o_ref[...] = _dot(a, w_ref[...].astype(BF16)) + b_ref[...]


def _ada(c_all, w_ada, b_ada):
    depth, d, nm = w_ada.shape
    tn = 1024
    return pl.pallas_call(
        _ada_kernel,
        grid=(depth, nm // tn),
        in_specs=[
            pl.BlockSpec((MOD_ROWS, d), lambda l, n: (0, 0)),
            pl.BlockSpec((None, d, tn), lambda l, n: (l, 0, n)),
            pl.BlockSpec((None, 1, tn), lambda l, n: (l, 0, n)),
        ],
        out_specs=pl.BlockSpec((None, MOD_ROWS, tn), lambda l, n: (l, 0, n)),
        out_shape=jax.ShapeDtypeStruct((depth, MOD_ROWS, nm), F32),
        compiler_params=_params("arbitrary", "arbitrary"),
        name="ada",
    )(c_all, w_ada, b_ada.reshape(depth, 1, nm))


def _mod_specs(tiles_per_batch, n_batch, d, ngrid):
    def batch_map(i, *_):
        return (i // tiles_per_batch, 0, 0)

    def ctx_map(i, *_):
        return (n_batch, 0, 0)

    del ngrid
    return (pl.BlockSpec((None, 1, d), batch_map), pl.BlockSpec((None, 1, d), ctx_map))


def _ffn_kernel(x_ref, sh_b, sh_c, sc_b, sc_c, ga_b, ga_c, g_ref, w1_ref, w3_ref, w2_ref,
                o_ref, h_ref, *, tiles_per_batch, ctx_len, nf):
    i = pl.program_id(0)
    f = pl.program_id(1)
    tm = x_ref.shape[0]
    is_ctx = _ctx_rows(i, tiles_per_batch, ctx_len, tm)

    @pl.when(f == 0)
    def _():
        x = x_ref[...]
        shift = _row_select(is_ctx, sh_c, sh_b)
        scale = _row_select(is_ctx, sc_c, sc_b)
        h_ref[...] = (_rms(x, g_ref[...]) * (1.0 + scale) + shift).astype(BF16)
        o_ref[...] = jnp.zeros_like(o_ref)

    h = h_ref[...]
    a = _dot(h, w1_ref[...])
    b = _dot(h, w3_ref[...])
    u = (_silu(a) * b).astype(BF16)
    o_ref[...] += _dot(u, w2_ref[...])

    @pl.when(f == nf - 1)
    def _():
        gate = _row_select(is_ctx, ga_c, ga_b)
        o_ref[...] = x_ref[...] + HALF_STEP * gate * o_ref[...]


def _ffn(x2, mods, g, w13, w2, *, tm, tiles_per_batch, n_batch, ctx_len):
    n, d = x2.shape
    fdim = w2.shape[0]
    tf = 512
    nf = fdim // tf
    shift, scale, gate = mods
    bs, cs = _mod_specs(tiles_per_batch, n_batch, d, 2)
    kern = functools.partial(_ffn_kernel, tiles_per_batch=tiles_per_batch, ctx_len=ctx_len, nf=nf)
    return pl.pallas_call(
        kern,
        grid=(n // tm, nf),
        in_specs=[
            pl.BlockSpec((tm, d), lambda i, f: (i, 0)),
            bs, cs, bs, cs, bs, cs,
            pl.BlockSpec((1, d), lambda i, f: (0, 0)),
            pl.BlockSpec((d, tf), lambda i, f: (0, f)),
            pl.BlockSpec((d, tf), lambda i, f: (0, f + nf)),
            pl.BlockSpec((tf, d), lambda i, f: (f, 0)),
        ],
        out_specs=pl.BlockSpec((tm, d), lambda i, f: (i, 0)),
        out_shape=jax.ShapeDtypeStruct((n, d), F32),
        scratch_shapes=[pltpu.VMEM((tm, d), BF16)],
        compiler_params=_params("arbitrary", "arbitrary"),
        name="ffn",
    )(x2, shift, shift, scale, scale, gate, gate, g.reshape(1, d), w13, w13, w2)


def _inproj_kernel(x_ref, sh_b, sh_c, sc_b, sc_c, g_ref, w_ref, o_ref, h_ref, *,
                   tiles_per_batch, ctx_len):
    i = pl.program_id(0)
    n = pl.program_id(1)
    tm = x_ref.shape[0]

    @pl.when(n == 0)
    def _():
        is_ctx = _ctx_rows(i, tiles_per_batch, ctx_len, tm)
        shift = _row_select(is_ctx, sh_c, sh_b)
        scale = _row_select(is_ctx, sc_c, sc_b)
        h_ref[...] = (_rms(x_ref[...], g_ref[...]) * (1.0 + scale) + shift).astype(BF16)

    o_ref[...] = _dot(h_ref[...], w_ref[...])


def _inproj(x2, mods, g, w, *, tm, tiles_per_batch, n_batch, ctx_len):
    n, d = x2.shape
    width = w.shape[1]
    tn = 1024
    shift, scale, _ = mods
    bs, cs = _mod_specs(tiles_per_batch, n_batch, d, 2)
    kern = functools.partial(_inproj_kernel, tiles_per_batch=tiles_per_batch, ctx_len=ctx_len)
    return pl.pallas_call(
        kern,
        grid=(n // tm, width // tn),
        in_specs=[
            pl.BlockSpec((tm, d), lambda i, j: (i, 0)),
            bs, cs, bs, cs,
            pl.BlockSpec((1, d), lambda i, j: (0, 0)),
            pl.BlockSpec((d, tn), lambda i, j: (0, j)),
        ],
        out_specs=pl.BlockSpec((tm, tn), lambda i, j: (i, j)),
        out_shape=jax.ShapeDtypeStruct((n, width), F32),
        scratch_shapes=[pltpu.VMEM((tm, d), BF16)],
        compiler_params=_params("arbitrary", "arbitrary"),
        name="inproj",
    )(x2, shift, shift, scale, scale, g.reshape(1, d), w)


def _ssd_block_of(d, j, nblk):
    return jnp.where(d == 0, j, jnp.where(j == 0, 0, nblk - j))


def _ssd_kernel(prev_ref, cur_ref, next_ref, z_ref, dt_ref, cw_ref, cb_ref, dtb_ref, alog_ref,
                dsk_ref, ng_ref, e_ref, o_ref, xw_ref, hs_ref, yf_ref, st_ref, *, nblk, mix):
    d = pl.program_id(1)
    j = pl.program_id(2)
    blk = _ssd_block_of(d, j, nblk)
    q = SSD_CHUNK
    gw = SSD_STATE
    hpg = mix // SSD_HEADDIM // SSD_GROUPS

    has_prev = jnp.logical_and(blk != 0, blk != 1)
    has_next = jnp.logical_and(blk != 0, blk != nblk - 1)
    xw_ref[0:SSD_HALO, :] = jnp.where(has_prev, prev_ref[...], 0.0)
    xw_ref[SSD_HALO:SSD_HALO + ROW_BLOCK, :] = cur_ref[...]
    xw_ref[SSD_HALO + ROW_BLOCK:, :] = jnp.where(has_next, next_ref[...], 0.0)
    pad = (SSD_CONV - 1) // 2
    acc = jnp.broadcast_to(cb_ref[...], (ROW_BLOCK, cb_ref.shape[1]))
    for k in range(SSD_CONV):
        acc = acc + cw_ref[k:k + 1, :] * xw_ref[pl.ds(SSD_HALO - pad + k, ROW_BLOCK), :]
    hs_ref[...] = _silu(acc)

    @pl.when(j == 0)
    def _():
        st_ref[...] = jnp.zeros_like(st_ref)

    ii = lax.broadcasted_iota(jnp.int32, (q, q), 0)
    jj = lax.broadcasted_iota(jnp.int32, (q, q), 1)
    sgn = 1 - 2 * d
    tri = (ii - jj) * sgn >= 0
    tri_b = jnp.where(tri, 1.0, 0.0).astype(BF16)
    lane = lax.broadcasted_iota(jnp.int32, (q, LANES), 1)
    lo_half = lane < SSD_HEADDIM

    a_neg = -jnp.exp(alog_ref[...])
    e01 = e_ref[...]

    for ci in range(ROW_BLOCK // q):
        chunk = jnp.where(d == 0, ci, ROW_BLOCK // q - 1 - ci)
        r0 = pl.multiple_of(chunk * q, q)
        x = hs_ref[pl.ds(r0, q), 0:mix]
        dt_raw = dt_ref[pl.ds(r0, q), :] + dtb_ref[...]
        dt = jnp.maximum(dt_raw, 0.0) + jnp.log1p(jnp.exp(-jnp.abs(dt_raw)))
        a = dt * a_neg
        cum = _exact_dot_left(tri_b, a)
        total = jnp.sum(a, axis=0, keepdims=True)
        cum_t = cum.T
        to_end = jnp.exp(total - cum)
        ecum = jnp.exp(cum)
        dt_x = _exact_dot_right(dt, e01)
        s_x = _exact_dot_right(dt * to_end, e01)
        ec_x = _exact_dot_right(ecum, e01)
        cd_x = _exact_dot_right(jnp.broadcast_to(jnp.exp(total), (SUBLANES, LANES)), e01)[0:1, :]
        xdt = x * dt_x
        xs = (x * s_x).astype(BF16)
        ys = []
        for g in range(SSD_GROUPS):
            bg = hs_ref[pl.ds(r0, q), mix + g * gw: mix + (g + 1) * gw]
            cg = hs_ref[pl.ds(r0, q), mix + (SSD_GROUPS + g) * gw: mix + (SSD_GROUPS + g + 1) * gw]
            bg_b = bg.astype(BF16)
            cg_b = cg.astype(BF16)
            cb = _dot_nt(cg_b, bg_b)
            c0 = g * hpg * SSD_HEADDIM
            cw = hpg * SSD_HEADDIM
            y_off = _dot(cg_b, st_ref[:, c0:c0 + cw].astype(BF16)) * ec_x[:, c0:c0 + cw]
            y_pairs = []
            for pr in range(hpg // 2):
                l0 = c0 + pr * LANES
                xp = xdt[:, l0:l0 + LANES]
                acc_p = None
                for half in range(2):
                    h = g * hpg + pr * 2 + half
                    seg = cum[:, h:h + 1] - cum_t[h:h + 1, :]
                    dec = jnp.exp(jnp.where(tri, seg, -jnp.inf))
                    m = (cb * dec).astype(BF16)
                    keep = lo_half if half == 0 else jnp.logical_not(lo_half)
                    xh = jnp.where(keep, xp, 0.0).astype(BF16)
                    t = _dot(m, xh)
                    acc_p = t if acc_p is None else acc_p + t
                y_pairs.append(acc_p)
            ys.append(jnp.concatenate(y_pairs, axis=1) + y_off)
            st_new = _dot(bg.T.astype(BF16), xs[:, c0:c0 + cw])
            st_ref[:, c0:c0 + cw] = st_ref[:, c0:c0 + cw] * cd_x[:, c0:c0 + cw] + st_new
        y = jnp.concatenate(ys, axis=1) + x * dsk_ref[...]
        row = pl.multiple_of(blk * ROW_BLOCK + r0, q)

        @pl.when(d == 0)
        def _():
            yf_ref[pl.ds(row, q), :] = y

        @pl.when(d == 1)
        def _():
            yt = (yf_ref[pl.ds(row, q), :] + y) * _silu(z_ref[pl.ds(r0, q), :])
            o_ref[pl.ds(r0, q), :] = _rms(yt, ng_ref[...]).astype(o_ref.dtype)


def _ssd(p3, conv_w, conv_b, dt_bias, a_log, d_skip, norm_g, lay, *, mix):
    nb, r, _ = p3.shape
    nblk = r // ROW_BLOCK
    cdim = conv_w.shape[1]
    heads = mix // SSD_HEADDIM
    hb = ROW_BLOCK // SSD_HALO

    def blk_map(b, d, j):
        return _ssd_block_of(d, j, nblk)

    pad_h = LANES - heads
    dtb = jnp.pad(dt_bias, ((0, 0), (0, pad_h))).reshape(2, 1, LANES)
    alog = jnp.pad(a_log, ((0, 0), (0, pad_h))).reshape(2, 1, LANES)
    dsk = jnp.repeat(d_skip, SSD_HEADDIM, axis=1).reshape(2, 1, mix)
    e01 = (jnp.arange(LANES)[:, None] == (jnp.arange(mix)[None, :] // SSD_HEADDIM)).astype(BF16)
    kern = functools.partial(_ssd_kernel, nblk=nblk, mix=mix)
    xbc_c, z_c, dt_c = lay["xbc"], lay["z"], lay["dt"]
    return pl.pallas_call(
        kern,
        grid=(nb, 2, nblk),
        in_specs=[
            pl.BlockSpec((None, SSD_HALO, cdim),
                         lambda b, d, j: (b, jnp.maximum(blk_map(b, d, j) * hb - 1, 0), xbc_c)),
            pl.BlockSpec((None, ROW_BLOCK, cdim), lambda b, d, j: (b, blk_map(b, d, j), xbc_c)),
            pl.BlockSpec((None, SSD_HALO, cdim),
                         lambda b, d, j: (b, jnp.minimum((blk_map(b, d, j) + 1) * hb, nblk * hb - 1), xbc_c)),
            pl.BlockSpec((None, ROW_BLOCK, mix), lambda b, d, j: (b, blk_map(b, d, j), z_c)),
            pl.BlockSpec((None, ROW_BLOCK, LANES), lambda b, d, j: (b, blk_map(b, d, j), dt_c)),
            pl.BlockSpec((SSD_CONV, cdim), lambda b, d, j: (0, 0)),
            pl.BlockSpec((1, cdim), lambda b, d, j: (0, 0)),
            pl.BlockSpec((None, 1, LANES), lambda b, d, j: (d, 0, 0)),
            pl.BlockSpec((None, 1, LANES), lambda b, d, j: (d, 0, 0)),
            pl.BlockSpec((None, 1, mix), lambda b, d, j: (d, 0, 0)),
            pl.BlockSpec((1, mix), lambda b, d, j: (0, 0)),
            pl.BlockSpec((LANES, mix), lambda b, d, j: (0, 0)),
        ],
        out_specs=pl.BlockSpec((None, ROW_BLOCK, mix),
                               lambda b, d, j: (b, jnp.where(d == 0, 0, blk_map(b, d, j)), 0)),
        out_shape=jax.ShapeDtypeStruct((nb, r, mix), BF16),
        scratch_shapes=[
            pltpu.VMEM((ROW_BLOCK + 2 * SSD_HALO, cdim), F32),
            pltpu.VMEM((ROW_BLOCK, cdim), F32),
            pltpu.VMEM((r, mix), F32),
            pltpu.VMEM((SSD_STATE, mix), F32),
        ],
        compiler_params=_params("arbitrary", "arbitrary", "arbitrary"),
        name="ssd",
    )(p3, p3, p3, p3, p3, conv_w, conv_b.reshape(1, cdim), dtb, alog, dsk,
      norm_g.reshape(1, mix), e01)


def _gmlp_kernel(p_ref, g_ref, ws_ref, bias_ref, o_ref, *, mix):
    p = p_ref[...]
    ge = 0.5 * p * (1.0 + jnp.tanh(math.sqrt(2.0 / math.pi) * (p + 0.044715 * (p * p * p))))
    u = ge[:, :mix]
    v = _rms(ge[:, mix:], g_ref[...]).astype(BF16)
    gw = mix // GMLP_GROUPS
    for c in range(ROW_BLOCK // GMLP_CHUNK):
        r0 = c * GMLP_CHUNK
        outs = []
        for g in range(GMLP_GROUPS):
            outs.append(_dot(ws_ref[g], v[r0:r0 + GMLP_CHUNK, g * gw:(g + 1) * gw]))
        mixed = jnp.concatenate(outs, axis=1) + bias_ref[...]
        o_ref[r0:r0 + GMLP_CHUNK, :] = (u[r0:r0 + GMLP_CHUNK, :] * mixed).astype(o_ref.dtype)


def _gmlp(p3, norm_g, ws, bs, lay, *, mix):
    nb, r, _ = p3.shape
    nblk = r // ROW_BLOCK
    bias_full = jnp.repeat(bs.T, mix // GMLP_GROUPS, axis=1)
    gm_c = lay["gm"]
    return pl.pallas_call(
        functools.partial(_gmlp_kernel, mix=mix),
        grid=(nb, nblk),
        in_specs=[
            pl.BlockSpec((None, ROW_BLOCK, 2 * mix), lambda b, j: (b, j, gm_c)),
            pl.BlockSpec((1, mix), lambda b, j: (0, 0)),
            pl.BlockSpec((GMLP_GROUPS, GMLP_CHUNK, GMLP_CHUNK), lambda b, j: (0, 0, 0)),
            pl.BlockSpec((GMLP_CHUNK, mix), lambda b, j: (0, 0)),
        ],
        out_specs=pl.BlockSpec((None, ROW_BLOCK, mix), lambda b, j: (b, j, 0)),
        out_shape=jax.ShapeDtypeStruct((nb, r, mix), BF16),
        compiler_params=_params("arbitrary", "arbitrary"),
        name="gmlp",
    )(p3, norm_g.reshape(1, mix), ws.astype(BF16), bias_full)


def _conf_kernel(prev_ref, cur_ref, next_ref, w_ref, b_ref, lg_ref, lb_ref, o_ref, gw_ref, *,
                 nblk, mix):
    j = pl.program_id(1)

    def glu(v):
        return v[:, :mix] * jax.nn.sigmoid(v[:, mix:])

    has_prev = jnp.logical_and(j != 0, j != 1)
    has_next = jnp.logical_and(j != 0, j != nblk - 1)
    gw_ref[0:CONF_HALO, :] = jnp.where(has_prev, glu(prev_ref[...]), 0.0)
    gw_ref[CONF_HALO:CONF_HALO + ROW_BLOCK, :] = glu(cur_ref[...])
    gw_ref[CONF_HALO + ROW_BLOCK:, :] = jnp.where(has_next, glu(next_ref[...]), 0.0)
    pad = (CONV_KERNEL - 1) // 2
    acc = jnp.broadcast_to(b_ref[...], (ROW_BLOCK, mix))
    for k in range(CONV_KERNEL):
        acc = acc + w_ref[k:k + 1, :] * gw_ref[pl.ds(CONF_HALO - pad + k, ROW_BLOCK), :]
    mu = jnp.mean(acc, axis=-1, keepdims=True)
    cen = acc - mu
    var = jnp.mean(cen * cen, axis=-1, keepdims=True)
    y = cen * lax.rsqrt(var + EPS) * lg_ref[...] + lb_ref[...]
    o_ref[...] = _silu(y).astype(o_ref.dtype)


def _conf(p3, dw_w, dw_b, ln_g, ln_b, lay, *, mix):
    nb, r, _ = p3.shape
    nblk = r // ROW_BLOCK
    hb = ROW_BLOCK // CONF_HALO
    cv_c = lay["cv"]
    return pl.pallas_call(
        functools.partial(_conf_kernel, nblk=nblk, mix=mix),
        grid=(nb, nblk),
        in_specs=[
            pl.BlockSpec((None, CONF_HALO, 2 * mix), lambda b, j: (b, jnp.maximum(j * hb - 1, 0), cv_c)),
            pl.BlockSpec((None, ROW_BLOCK, 2 * mix), lambda b, j: (b, j, cv_c)),
            pl.BlockSpec((None, CONF_HALO, 2 * mix),
                         lambda b, j: (b, jnp.minimum((j + 1) * hb, nblk * hb - 1), cv_c)),
            pl.BlockSpec((CONV_KERNEL, mix), lambda b, j: (0, 0)),
            pl.BlockSpec((1, mix), lambda b, j: (0, 0)),
            pl.BlockSpec((1, mix), lambda b, j: (0, 0)),
            pl.BlockSpec((1, mix), lambda b, j: (0, 0)),
        ],
        out_specs=pl.BlockSpec((None, ROW_BLOCK, mix), lambda b, j: (b, j, 0)),
        out_shape=jax.ShapeDtypeStruct((nb, r, mix), BF16),
        scratch_shapes=[pltpu.VMEM((ROW_BLOCK + 2 * CONF_HALO, mix), F32)],
        compiler_params=_params("arbitrary", "arbitrary"),
        name="conf",
    )(p3, p3, p3, dw_w, dw_b.reshape(1, mix), ln_g.reshape(1, mix), ln_b.reshape(1, mix))


def _rope(x, cos, sin_signed):
    lane = lax.broadcasted_iota(jnp.int32, x.shape, 1)
    first = (lane & (HEAD_DIM // 4)) == 0
    partner = jnp.where(first, pltpu.roll(x, HEAD_DIM - HEAD_DIM // 4, 1), pltpu.roll(x, HEAD_DIM // 4, 1))
    return x * cos + partner * sin_signed


def _attn_kernel(sink_ref, q_ref, k_ref, v_ref, cos_ref, sin_ref, o_ref, kr_ref, vb_ref, *,
                 ctx_len, n_kv, rows):
    j = pl.program_id(1)
    blk = WINDOW
    scale = HEAD_DIM ** -0.5

    @pl.when(j == 0)
    def _():
        for g in range(n_kv):
            kr_ref[:, g * HEAD_DIM:(g + 1) * HEAD_DIM] = _rope(
                k_ref[:, g * HEAD_DIM:(g + 1) * HEAD_DIM], cos_ref[...], sin_ref[...]).astype(BF16)
        vb_ref[...] = v_ref[...].astype(BF16)

    q0 = pl.multiple_of(j * blk, blk)
    cos_q = cos_ref[pl.ds(q0, blk), :]
    sin_q = sin_ref[pl.ds(q0, blk), :]
    band0 = pl.multiple_of(jnp.clip((j - 1) * blk, 0, rows - 3 * blk), blk)
    qpos = q0 + lax.broadcasted_iota(jnp.int32, (KV_RATIO * blk, 3 * blk), 0) % blk
    kpos = band0 + lax.broadcasted_iota(jnp.int32, (KV_RATIO * blk, 3 * blk), 1)
    ok = jnp.logical_and(jnp.logical_and(qpos >= ctx_len, kpos >= ctx_len), jnp.abs(kpos - qpos) <= WINDOW)
    hrow = lax.broadcasted_iota(jnp.int32, (KV_RATIO * blk, 1), 0) // blk

    for g in range(n_kv):
        qs = []
        for r in range(KV_RATIO):
            h = g * KV_RATIO + r
            qs.append(_rope(q_ref[:, h * HEAD_DIM:(h + 1) * HEAD_DIM], cos_q, sin_q).astype(BF16))
        qg = jnp.concatenate(qs, axis=0)
        sink = jnp.zeros((KV_RATIO * blk, 1), F32)
        for r in range(KV_RATIO):
            sink = jnp.where(hrow == r, sink_ref[g * KV_RATIO + r], sink)
        gc = slice(g * HEAD_DIM, (g + 1) * HEAD_DIM)
        k_band = kr_ref[pl.ds(band0, 3 * blk), gc]
        v_band = vb_ref[pl.ds(band0, 3 * blk), gc]
        k_ctx = kr_ref[0:ctx_len, gc]
        v_ctx = vb_ref[0:ctx_len, gc]
        s_band = jnp.where(ok, _dot_nt(qg, k_band) * scale, -jnp.inf)
        s_ctx = _dot_nt(qg, k_ctx) * scale
        m = jnp.maximum(jnp.maximum(jnp.max(s_band, axis=-1, keepdims=True),
                                    jnp.max(s_ctx, axis=-1, keepdims=True)), sink)
        p_band = jnp.exp(s_band - m)
        p_ctx = jnp.exp(s_ctx - m)
        den = (jnp.sum(p_band, axis=-1, keepdims=True) + jnp.sum(p_ctx, axis=-1, keepdims=True)
               + jnp.exp(sink - m))
        o = (_dot(p_band.astype(BF16), v_band) + _dot(p_ctx.astype(BF16), v_ctx)) / den
        for r in range(KV_RATIO):
            h = g * KV_RATIO + r
            o_ref[:, h * HEAD_DIM:(h + 1) * HEAD_DIM] = o[r * blk:(r + 1) * blk, :].astype(o_ref.dtype)


def _attn(p3, sink, cos_t, sin_t, lay, *, mix, ctx_len):
    nb, r, _ = p3.shape
    n_heads = mix // HEAD_DIM
    n_kv = n_heads // KV_RATIO
    kvw = n_kv * HEAD_DIM
    q_c, k_c, v_c = lay["q"], lay["k"], lay["v"]
    kern = functools.partial(_attn_kernel, ctx_len=ctx_len, n_kv=n_kv, rows=r)
    return pl.pallas_call(
        kern,
        grid=(nb, r // WINDOW),
        in_specs=[
            pl.BlockSpec(memory_space=pltpu.SMEM),
            pl.BlockSpec((None, WINDOW, mix), lambda b, j: (b, j, q_c)),
            pl.BlockSpec((None, r, kvw), lambda b, j: (b, 0, k_c)),
            pl.BlockSpec((None, r, kvw), lambda b, j: (b, 0, v_c)),
            pl.BlockSpec((r, HEAD_DIM), lambda b, j: (0, 0)),
            pl.BlockSpec((r, HEAD_DIM), lambda b, j: (0, 0)),
        ],
        out_specs=pl.BlockSpec((None, WINDOW, mix), lambda b, j: (b, j, 0)),
        out_shape=jax.ShapeDtypeStruct((nb, r, mix), BF16),
        scratch_shapes=[pltpu.VMEM((r, kvw), BF16), pltpu.VMEM((r, kvw), BF16)],
        compiler_params=_params("arbitrary", "arbitrary"),
        name="attn",
    )(sink, p3, p3, p3, cos_t, sin_t)


def _rope_tables(seq, ctx_len):
    quarter = HEAD_DIM // 4
    t = jnp.arange(seq, dtype=jnp.int32)
    rows = (t // GRID_W).astype(F32)
    cols = (t % GRID_W).astype(F32)
    freqs = ROPE_BASE ** (-jnp.arange(quarter, dtype=F32) / quarter)
    ang_r = rows[:, None] * freqs
    ang_c = cols[:, None] * freqs
    cos = jnp.concatenate([jnp.cos(ang_r)] * 2 + [jnp.cos(ang_c)] * 2, axis=1)
    sin = jnp.concatenate([-jnp.sin(ang_r), jnp.sin(ang_r), -jnp.sin(ang_c), jnp.sin(ang_c)], axis=1)
    cos = jnp.concatenate([jnp.ones((ctx_len, HEAD_DIM), F32), cos], axis=0)
    sin = jnp.concatenate([jnp.zeros((ctx_len, HEAD_DIM), F32), sin], axis=0)
    return cos, sin


def _merge_kernel(x_ref, sh_b, sh_c, sc_b, sc_c, g_ref, ya_ref, yb_ref, yc_ref, yd_ref,
                  wg_ref, bg_ref, wb_ref, o_ref, h_ref, *, tiles_per_batch, ctx_len):
    i = pl.program_id(0)
    n = pl.program_id(1)
    tm = x_ref.shape[0]

    @pl.when(n == 0)
    def _():
        is_ctx = _ctx_rows(i, tiles_per_batch, ctx_len, tm)
        shift = _row_select(is_ctx, sh_c, sh_b)
        scale = _row_select(is_ctx, sc_c, sc_b)
        h_ref[...] = (_rms(x_ref[...], g_ref[...]) * (1.0 + scale) + shift).astype(BF16)

    h = h_ref[...]
    acc = None
    for b, y_ref in enumerate((ya_ref, yb_ref, yc_ref, yd_ref)):
        gate = jax.nn.sigmoid(_dot(h, wg_ref[b]) + bg_ref[b])
        term = gate * _dot(y_ref[...], wb_ref[b])
        acc = term if acc is None else acc + term
    o_ref[...] = acc.astype(o_ref.dtype)


def _merge(x2, mods, g, ys, w_gate, b_gate, w_branch, *, tm, tiles_per_batch, n_batch, ctx_len):
    n, d = x2.shape
    mix = w_branch.shape[1]
    tn = 256
    shift, scale, _ = mods
    bs, cs = _mod_specs(tiles_per_batch, n_batch, d, 2)
    y_spec = pl.BlockSpec((tm, mix), lambda i, j: (i, 0))
    kern = functools.partial(_merge_kernel, tiles_per_batch=tiles_per_batch, ctx_len=ctx_len)
    return pl.pallas_call(
        kern,
        grid=(n // tm, d // tn),
        in_specs=[
            pl.BlockSpec((tm, d), lambda i, j: (i, 0)),
            bs, cs, bs, cs,
            pl.BlockSpec((1, d), lambda i, j: (0, 0)),
            y_spec, y_spec, y_spec, y_spec,
            pl.BlockSpec((N_BRANCH, d, tn), lambda i, j: (0, 0, j)),
            pl.BlockSpec((N_BRANCH, 1, tn), lambda i, j: (0, 0, j)),
            pl.BlockSpec((N_BRANCH, mix, tn), lambda i, j: (0, 0, j)),
        ],
        out_specs=pl.BlockSpec((tm, tn), lambda i, j: (i, j)),
        out_shape=jax.ShapeDtypeStruct((n, d), BF16),
        scratch_shapes=[pltpu.VMEM((tm, d), BF16)],
        compiler_params=_params("arbitrary", "arbitrary"),
        name="merge",
    )(x2, shift, shift, scale, scale, g.reshape(1, d), *ys, w_gate, b_gate.reshape(N_BRANCH, 1, d),
      w_branch)


def _outproj_kernel(x_ref, ga_b, ga_c, m_ref, w_ref, o_ref, *, tiles_per_batch, ctx_len):
    i = pl.program_id(0)
    tm = x_ref.shape[0]
    is_ctx = _ctx_rows(i, tiles_per_batch, ctx_len, tm)
    gate = _row_select(is_ctx, ga_c, ga_b)
    o_ref[...] = x_ref[...] + gate * _dot(m_ref[...], w_ref[...])


def _outproj(x2, mods, merged, w_out, *, tm, tiles_per_batch, n_batch, ctx_len):
    n, d = x2.shape
    tn = 512
    _, _, gate = mods

    def batch_map(i, j):
        return (i // tiles_per_batch, 0, j)

    def ctx_map(i, j):
        return (n_batch, 0, j)

    kern = functools.partial(_outproj_kernel, tiles_per_batch=tiles_per_batch, ctx_len=ctx_len)
    return pl.pallas_call(
        kern,
        grid=(n // tm, d // tn),
        in_specs=[
            pl.BlockSpec((tm, tn), lambda i, j: (i, j)),
            pl.BlockSpec((None, 1, tn), batch_map),
            pl.BlockSpec((None, 1, tn), ctx_map),
            pl.BlockSpec((tm, d), lambda i, j: (i, 0)),
            pl.BlockSpec((d, tn), lambda i, j: (0, j)),
        ],
        out_specs=pl.BlockSpec((tm, tn), lambda i, j: (i, j)),
        out_shape=jax.ShapeDtypeStruct((n, d), F32),
        compiler_params=_params("arbitrary", "arbitrary"),
        name="outproj",
    )(x2, gate, gate, merged, w_out)


def _final_kernel(x_ref, g_ref, o_ref):
    o_ref[...] = _rms(x_ref[...], g_ref[...])


def _final(x3, g, *, ctx_len, seq):
    nb, _, d = x3.shape
    off = ctx_len // ROW_BLOCK
    return pl.pallas_call(
        _final_kernel,
        grid=(nb, seq // ROW_BLOCK),
        in_specs=[
            pl.BlockSpec((None, ROW_BLOCK, d), lambda b, j: (b, j + off, 0)),
            pl.BlockSpec((1, d), lambda b, j: (0, 0)),
        ],
        out_specs=pl.BlockSpec((None, ROW_BLOCK, d), lambda b, j: (b, j, 0)),
        out_shape=jax.ShapeDtypeStruct((nb, seq, d), F32),
        compiler_params=_params("arbitrary", "arbitrary"),
        name="final",
    )(x3, g.reshape(1, d))


def _proj_layout(mix, n_kv):
    cdim = mix + 2 * SSD_GROUPS * SSD_STATE
    heads = mix // SSD_HEADDIM
    kvw = n_kv * HEAD_DIM
    sizes = dict(z=mix, xbc=cdim, dt=heads, gm=2 * mix, cv=2 * mix, q=mix, k=kvw, v=kvw)
    src, o = {}, 0
    for name in ("z", "xbc", "dt", "gm", "cv", "q", "k", "v"):
        src[name] = (o, sizes[name])
        o += sizes[name]
    order = ("xbc", "gm", "cv", "z", "q", "k", "v", "dt")
    dst, o = {}, 0
    for name in order:
        width = LANES if name == "dt" else sizes[name]
        assert o % width == 0
        dst[name] = o // width
        o += width
    total = -(-o // 1024) * 1024
    return src, order, dst, o, total


def _permute_w_in(w_in_l, src, order, used, total):
    cols = []
    for name in order:
        s, w = src[name]
        cols.append(w_in_l[:, s:s + w])
    out = jnp.concatenate(cols, axis=1)
    return jnp.pad(out, ((0, 0), (0, total - out.shape[1]))).astype(BF16)


def kernel(x, c, ctx, c_ctx, w_ada, b_ada, ffn1_norm, ffn1_w13, ffn1_w2, mix_norm, w_in, w_gate, b_gate, w_branch, w_out, ssd_conv_w, ssd_conv_b, ssd_dt_bias, ssd_a_log, ssd_d, ssd_norm, gmlp_norm, gmlp_ws, gmlp_bs, conv_dw_w, conv_dw_b, conv_ln_g, conv_ln_b, attn_sink, ffn2_norm, ffn2_w13, ffn2_w2, final_norm):
    nb, seq, d = x.shape
    ctx_len = ctx.shape[1]
    depth = w_ada.shape[0]
    mix = w_branch.shape[2]
    n_kv = mix // HEAD_DIM // KV_RATIO
    r = ctx_len + seq
    assert ctx_len == ROW_BLOCK and seq % ROW_BLOCK == 0 and nb + 1 <= MOD_ROWS
    tm = 768
    assert r % tm == 0
    tiles_per_batch = r // tm
    tile_kw = dict(tm=tm, tiles_per_batch=tiles_per_batch, n_batch=nb, ctx_len=ctx_len)

    src, order, dst, used, total = _proj_layout(mix, n_kv)
    cos_t, sin_t = _rope_tables(seq, ctx_len)

    c_all = jnp.concatenate([c, c_ctx[None, :], jnp.zeros((MOD_ROWS - nb - 1, d), F32)], axis=0)
    mod_all = _ada(c_all, w_ada, b_ada).reshape(depth, MOD_ROWS, N_MOD, 1, d)

    xcur = jnp.concatenate([ctx, x], axis=1).reshape(nb * r, d)
    for l in range(depth):
        def mods(k, l=l):
            return tuple(mod_all[l, :, 3 * k + t] for t in range(3))

        xcur = _ffn(xcur, mods(0), ffn1_norm[l], ffn1_w13[l].astype(BF16), ffn1_w2[l].astype(BF16),
                    **tile_kw)
        w_in_l = _permute_w_in(w_in[l], src, order, used, total)
        p3 = _inproj(xcur, mods(1), mix_norm[l], w_in_l, **tile_kw).reshape(nb, r, total)
        ya = _ssd(p3, ssd_conv_w[l], ssd_conv_b[l], ssd_dt_bias[l], ssd_a_log[l], ssd_d[l],
                  ssd_norm[l], dst, mix=mix)
        yb = _gmlp(p3, gmlp_norm[l], gmlp_ws[l], gmlp_bs[l], dst, mix=mix)
        yc = _conf(p3, conv_dw_w[l], conv_dw_b[l], conv_ln_g[l], conv_ln_b[l], dst, mix=mix)
        yd = _attn(p3, attn_sink[l], cos_t, sin_t, dst, mix=mix, ctx_len=ctx_len)
        ys = tuple(y.reshape(nb * r, mix) for y in (ya, yb, yc, yd))
        merged = _merge(xcur, mods(1), mix_norm[l], ys, w_gate[l].astype(BF16), b_gate[l],
                        w_branch[l].astype(BF16), **tile_kw)
        xcur = _outproj(xcur, mods(1), merged, w_out[l].astype(BF16), **tile_kw)
        xcur = _ffn(xcur, mods(2), ffn2_norm[l], ffn2_w13[l].astype(BF16), ffn2_w2[l].astype(BF16),
                    **tile_kw)
    return _final(xcur.reshape(nb, r, d), final_norm, ctx_len=ctx_len, seq=seq)
```

```python
import functools
import math

import jax
import jax.numpy as jnp
from jax import lax
from jax.experimental import pallas as pl
from jax.experimental.pallas import tpu as pltpu

F32 = jnp.float32
BF16 = jnp.bfloat16

GRID_W = 64
N_BRANCH = 4
N_MOD = 9
HALF_STEP = 0.5
SSD_HEADDIM = 64
SSD_GROUPS = 4
SSD_STATE = 128
SSD_CONV = 5
SSD_CHUNK = 128
GMLP_GROUPS = 8
GMLP_CHUNK = 128
CONV_KERNEL = 31
HEAD_DIM = 128
KV_RATIO = 4
WINDOW = 128
ROPE_BASE = 10000.0
EPS = 1e-6

LANES = 128
SUBLANES = 8
VMEM_LIMIT_BYTES = 56 * 1024 * 1024

TOKEN_TILE = 768
FFN_TF = 512
INPROJ_TN = 1024
MERGE_TN = 256
OUT_TN = 512
ADA_TN = 1024
ROW_BLOCK = 256
SSD_HALO = 8
CONF_HALO = 16
MOD_ROWS = 24


def _params(*sem):
    return pltpu.CompilerParams(dimension_semantics=sem, vmem_limit_bytes=VMEM_LIMIT_BYTES)


def _silu(x):
    return x * jax.nn.sigmoid(x)


def _dot(a, b):
    return jnp.dot(a, b, preferred_element_type=F32)


def _dot_nt(a, b):
    return lax.dot_general(a, b, (((1,), (1,)), ((), ())), preferred_element_type=F32)


def _rms(x, g):
    return x * lax.rsqrt(jnp.mean(x * x, axis=-1, keepdims=True) + EPS) * g


def _split3(v):
    hi = v.astype(BF16)
    r1 = v - hi.astype(F32)
    mid = r1.astype(BF16)
    lo = (r1 - mid.astype(F32)).astype(BF16)
    return hi, mid, lo


def _exact_dot_left(m01, v):
    hi, mid, lo = _split3(v)
    return _dot(m01, hi) + _dot(m01, mid) + _dot(m01, lo)


def _row_parts(tile, tiles_per_batch, ctx_len, tm, ctx_ref, batch_ref):
    first = tile % tiles_per_batch == 0
    top = jnp.where(first, ctx_ref[...], batch_ref[...])
    return ((slice(0, ctx_len), top), (slice(ctx_len, tm), batch_ref[...]))


def _modnorm_store(h_ref, x_ref, g_ref, tile, tiles_per_batch, ctx_len, sh_b, sh_c, sc_b, sc_c):
    tm = x_ref.shape[0]
    shifts = _row_parts(tile, tiles_per_batch, ctx_len, tm, sh_c, sh_b)
    scales = _row_parts(tile, tiles_per_batch, ctx_len, tm, sc_c, sc_b)
    for (rows, shift), (_, scale) in zip(shifts, scales):
        h_ref[rows, :] = (_rms(x_ref[rows, :], g_ref[...]) * (1.0 + scale) + shift).astype(h_ref.dtype)


def _ada_kernel(c_ref, w_ref, b_ref, o_ref):
    a = _silu(c_ref[...]).astype(BF16)
    o_ref[...] = _dot(a, w_ref[...].astype(BF16)) + b_ref[...]


def _ada(c_all, w_ada, b_ada):
    depth, d, nm = w_ada.shape
    tn = ADA_TN
    return pl.pallas_call(
        _ada_kernel,
        grid=(depth, nm // tn),
        in_specs=[
            pl.BlockSpec((MOD_ROWS, d), lambda l, n: (0, 0)),
            pl.BlockSpec((None, d, tn), lambda l, n: (l, 0, n)),
            pl.BlockSpec((None, 1, tn), lambda l, n: (l, 0, n)),
        ],
        out_specs=pl.BlockSpec((None, MOD_ROWS, tn), lambda l, n: (l, 0, n)),
        out_shape=jax.ShapeDtypeStruct((depth, MOD_ROWS, nm), F32),
        compiler_params=_params("arbitrary", "arbitrary"),
        name="ada",
    )(c_all, w_ada, b_ada.reshape(depth, 1, nm))


def _mod_specs(tiles_per_batch, n_batch, d):
    def batch_map(i, *_):
        return (i // tiles_per_batch, 0, 0)

    def ctx_map(i, *_):
        return (n_batch, 0, 0)

    return (pl.BlockSpec((None, 1, d), batch_map), pl.BlockSpec((None, 1, d), ctx_map))


def _ffn_kernel(x_ref, sh_b, sh_c, sc_b, sc_c, ga_b, ga_c, g_ref, w1_ref, w3_ref, w2_ref,
                o_ref, h_ref, *, tiles_per_batch, ctx_len, nf):
    i = pl.program_id(0)
    f = pl.program_id(1)
    tm = x_ref.shape[0]

    @pl.when(f == 0)
    def _():
        _modnorm_store(h_ref, x_ref, g_ref, i, tiles_per_batch, ctx_len, sh_b, sh_c, sc_b, sc_c)
        o_ref[...] = jnp.zeros_like(o_ref)

    h = h_ref[...]
    a = _dot(h, w1_ref[...])
    b = _dot(h, w3_ref[...])
    u = (_silu(a) * b).astype(BF16)
    o_ref[...] += _dot(u, w2_ref[...])

    @pl.when(f == nf - 1)
    def _():
        for rows, gate in _row_parts(i, tiles_per_batch, ctx_len, tm, ga_c, ga_b):
            o_ref[rows, :] = x_ref[rows, :] + HALF_STEP * gate * o_ref[rows, :]


def _ffn(x2, mods, g, w13, w2, layer, *, tm, tiles_per_batch, n_batch, ctx_len):
    n, d = x2.shape
    fdim = w2.shape[1]
    tf = FFN_TF
    nf = fdim // tf
    shift, scale, gate = mods
    bs, cs = _mod_specs(tiles_per_batch, n_batch, d)
    kern = functools.partial(_ffn_kernel, tiles_per_batch=tiles_per_batch, ctx_len=ctx_len, nf=nf)
    return pl.pallas_call(
        kern,
        grid=(n // tm, nf),
        in_specs=[
            pl.BlockSpec((tm, d), lambda i, f: (i, 0)),
            bs, cs, bs, cs, bs, cs,
            pl.BlockSpec((1, d), lambda i, f: (0, 0)),
            pl.BlockSpec((None, d, tf), lambda i, f: (layer, 0, f)),
            pl.BlockSpec((None, d, tf), lambda i, f: (layer, 0, f + nf)),
            pl.BlockSpec((None, tf, d), lambda i, f: (layer, f, 0)),
        ],
        out_specs=pl.BlockSpec((tm, d), lambda i, f: (i, 0)),
        out_shape=jax.ShapeDtypeStruct((n, d), F32),
        scratch_shapes=[pltpu.VMEM((tm, d), BF16)],
        compiler_params=_params("arbitrary", "arbitrary"),
        name="ffn",
    )(x2, shift, shift, scale, scale, gate, gate, g.reshape(1, d), w13, w13, w2)


def _inproj_kernel(x_ref, sh_b, sh_c, sc_b, sc_c, g_ref, w_ref, o_ref, h_ref, *,
                   tiles_per_batch, ctx_len):
    i = pl.program_id(0)
    n = pl.program_id(1)

    @pl.when(n == 0)
    def _():
        _modnorm_store(h_ref, x_ref, g_ref, i, tiles_per_batch, ctx_len, sh_b, sh_c, sc_b, sc_c)

    o_ref[...] = _dot(h_ref[...], w_ref[...])


def _inproj(x2, mods, g, w, layer, *, tm, tiles_per_batch, n_batch, ctx_len):
    n, d = x2.shape
    width = w.shape[2]
    tn = INPROJ_TN
    shift, scale, _ = mods
    bs, cs = _mod_specs(tiles_per_batch, n_batch, d)
    kern = functools.partial(_inproj_kernel, tiles_per_batch=tiles_per_batch, ctx_len=ctx_len)
    return pl.pallas_call(
        kern,
        grid=(n // tm, width // tn),
        in_specs=[
            pl.BlockSpec((tm, d), lambda i, j: (i, 0)),
            bs, cs, bs, cs,
            pl.BlockSpec((1, d), lambda i, j: (0, 0)),
            pl.BlockSpec((None, d, tn), lambda i, j: (layer, 0, j)),
        ],
        out_specs=[pl.BlockSpec((tm, tn), lambda i, j: (i, j)),
                   pl.BlockSpec((tm, d), lambda i, j: (i, 0))],
        out_shape=[jax.ShapeDtypeStruct((n, width), F32), jax.ShapeDtypeStruct((n, d), BF16)],
        compiler_params=_params("arbitrary", "arbitrary"),
        name="inproj",
    )(x2, shift, shift, scale, scale, g.reshape(1, d), w)


def _ssd_block_of(d, j, nblk):
    return jnp.where(d == 0, j, jnp.where(j == 0, 0, nblk - j))


def _ssd_kernel(prev_ref, cur_ref, next_ref, z_ref, dt_ref, cw_ref, cb_ref, dtb_ref, alog_ref,
                dsk_ref, ng_ref, e3_ref, o_ref, xw_ref, x_ref, b_ref, c_ref, bt_ref, yf_ref, yc_ref, st_ref,
                *, nblk, mix):
    d = pl.program_id(1)
    j = pl.program_id(2)
    blk = _ssd_block_of(d, j, nblk)
    q = SSD_CHUNK
    gw = SSD_STATE
    bcw = SSD_GROUPS * gw
    hpg = mix // SSD_HEADDIM // SSD_GROUPS
    nch = ROW_BLOCK // q
    row0 = pl.multiple_of(blk * ROW_BLOCK, ROW_BLOCK)

    @pl.when(d == 0)
    def _():
        has_prev = jnp.logical_and(blk != 0, blk != 1)
        has_next = jnp.logical_and(blk != 0, blk != nblk - 1)
        xw_ref[0:SSD_HALO, :] = jnp.where(has_prev, prev_ref[...], 0.0)
        xw_ref[SSD_HALO:SSD_HALO + ROW_BLOCK, :] = cur_ref[...]
        xw_ref[SSD_HALO + ROW_BLOCK:, :] = jnp.where(has_next, next_ref[...], 0.0)
        pad = (SSD_CONV - 1) // 2
        acc = jnp.broadcast_to(cb_ref[...], (ROW_BLOCK, cb_ref.shape[1]))
        for k in range(SSD_CONV):
            acc = acc + cw_ref[k:k + 1, :] * xw_ref[pl.ds(SSD_HALO - pad + k, ROW_BLOCK), :]
        hs = _silu(acc)
        x_ref[pl.ds(row0, ROW_BLOCK), :] = hs[:, 0:mix]
        b_ref[pl.ds(row0, ROW_BLOCK), :] = hs[:, mix:mix + bcw].astype(BF16)
        c_ref[pl.ds(row0, ROW_BLOCK), :] = hs[:, mix + bcw:mix + 2 * bcw].astype(BF16)
        for ci in range(nch):
            for g in range(SSD_GROUPS):
                bt_ref[blk * nch + ci, g * gw:(g + 1) * gw, :] = (
                    hs[ci * q:(ci + 1) * q, mix + g * gw:mix + (g + 1) * gw].T.astype(BF16))

    @pl.when(j == 0)
    def _():
        st_ref[...] = jnp.zeros_like(st_ref)

    ii = lax.broadcasted_iota(jnp.int32, (q, q), 0)
    jj = lax.broadcasted_iota(jnp.int32, (q, q), 1)
    sgn = 1 - 2 * d
    tri = (ii - jj) * sgn >= 0
    tri_b = jnp.where(tri, 1.0, 0.0).astype(BF16)
    lane = lax.broadcasted_iota(jnp.int32, (q, LANES), 1)
    lo_half = lane < SSD_HEADDIM

    a_neg = -jnp.exp(alog_ref[...])
    e3 = e3_ref[...]

    for ci in range(nch):
        chunk = jnp.where(d == 0, ci, nch - 1 - ci)
        r0 = pl.multiple_of(chunk * q, q)
        grow = pl.multiple_of(row0 + r0, q)
        x = x_ref[pl.ds(grow, q), :]
        dt_raw = dt_ref[pl.ds(r0, q), :] + dtb_ref[...]
        dt = jnp.maximum(dt_raw, 0.0) + jnp.log1p(jnp.exp(-jnp.abs(dt_raw)))
        a = dt * a_neg
        cum = _exact_dot_left(tri_b, a)
        total = jnp.sum(a, axis=0, keepdims=True)
        cum_t = cum.T
        to_end = jnp.exp(total - cum)
        ecum = jnp.exp(cum)
        v = jnp.concatenate([dt, dt * to_end, ecum, jnp.broadcast_to(jnp.exp(total), (SUBLANES, LANES))], axis=0)
        ex = _dot(jnp.concatenate(_split3(v), axis=1), e3)
        dt_x, s_x, ec_x, cd_x = ex[0:q], ex[q:2 * q], ex[2 * q:3 * q], ex[3 * q:3 * q + 1]
        xdt = x * dt_x
        xs = (x * s_x).astype(BF16)
        ys = []
        for g in range(SSD_GROUPS):
            bg_b = b_ref[pl.ds(grow, q), g * gw:(g + 1) * gw]
            cg_b = c_ref[pl.ds(grow, q), g * gw:(g + 1) * gw]
            btg = bt_ref[blk * nch + chunk, g * gw:(g + 1) * gw, :]
            cb = _dot_nt(cg_b, bg_b)
            c0 = g * hpg * SSD_HEADDIM
            cw = hpg * SSD_HEADDIM
            y_off = _dot(cg_b, st_ref[:, c0:c0 + cw].astype(BF16)) * ec_x[:, c0:c0 + cw]
            y_pairs = []
            for pr in range(hpg // 2):
                l0 = c0 + pr * LANES
                xp = xdt[:, l0:l0 + LANES]
                acc_p = None
                for half in range(2):
                    h = g * hpg + pr * 2 + half
                    seg = cum[:, h:h + 1] - cum_t[h:h + 1, :]
                    dec = jnp.exp(jnp.where(tri, seg, -jnp.inf))
                    m = (cb * dec).astype(BF16)
                    keep = lo_half if half == 0 else jnp.logical_not(lo_half)
                    xh = jnp.where(keep, xp, 0.0).astype(BF16)
                    t = _dot(m, xh)
                    acc_p = t if acc_p is None else acc_p + t
                y_pairs.append(acc_p)
            ys.append(jnp.concatenate(y_pairs, axis=1) + y_off)
            st_new = _dot(btg, xs[:, c0:c0 + cw])
            st_ref[:, c0:c0 + cw] = st_ref[:, c0:c0 + cw] * cd_x[:, c0:c0 + cw] + st_new
        yc_ref[pl.ds(r0, q), :] = jnp.concatenate(ys, axis=1) + x * dsk_ref[...]

    @pl.when(d == 0)
    def _():
        yf_ref[pl.ds(row0, ROW_BLOCK), :] = yc_ref[...]

    @pl.when(d == 1)
    def _():
        yt = (yf_ref[pl.ds(row0, ROW_BLOCK), :] + yc_ref[...]) * _silu(z_ref[...])
        o_ref[...] = _rms(yt, ng_ref[...]).astype(o_ref.dtype)


def _ssd(p3, conv_w, conv_b, dt_bias, a_log, d_skip, norm_g, lay, *, mix):
    nb, r, _ = p3.shape
    nblk = r // ROW_BLOCK
    cdim = conv_w.shape[1]
    heads = mix // SSD_HEADDIM
    hb = ROW_BLOCK // SSD_HALO
    bcw = SSD_GROUPS * SSD_STATE

    def blk_map(b, d, j):
        return _ssd_block_of(d, j, nblk)

    def conv_blk(b, d, j):
        return jnp.where(d == 0, j, nblk - 1)

    pad_h = LANES - heads
    dtb = jnp.pad(dt_bias, ((0, 0), (0, pad_h))).reshape(2, 1, LANES)
    alog = jnp.pad(a_log, ((0, 0), (0, pad_h))).reshape(2, 1, LANES)
    dsk = jnp.repeat(d_skip, SSD_HEADDIM, axis=1).reshape(2, 1, mix)
    e01 = (jnp.arange(LANES)[:, None] == (jnp.arange(mix)[None, :] // SSD_HEADDIM)).astype(BF16)
    e3 = jnp.concatenate([e01, e01, e01], axis=0)
    kern = functools.partial(_ssd_kernel, nblk=nblk, mix=mix)
    xbc_c, z_c, dt_c = lay["xbc"], lay["z"], lay["dt"]
    return pl.pallas_call(
        kern,
        grid=(nb, 2, nblk),
        in_specs=[
            pl.BlockSpec((None, SSD_HALO, cdim),
                         lambda b, d, j: (b, jnp.maximum(conv_blk(b, d, j) * hb - 1, 0), xbc_c)),
            pl.BlockSpec((None, ROW_BLOCK, cdim), lambda b, d, j: (b, conv_blk(b, d, j), xbc_c)),
            pl.BlockSpec((None, SSD_HALO, cdim),
                         lambda b, d, j: (b, jnp.minimum((conv_blk(b, d, j) + 1) * hb, nblk * hb - 1), xbc_c)),
            pl.BlockSpec((None, ROW_BLOCK, mix),
                         lambda b, d, j: (b, jnp.where(d == 0, 0, blk_map(b, d, j)), z_c)),
            pl.BlockSpec((None, ROW_BLOCK, LANES), lambda b, d, j: (b, blk_map(b, d, j), dt_c)),
            pl.BlockSpec((SSD_CONV, cdim), lambda b, d, j: (0, 0)),
            pl.BlockSpec((1, cdim), lambda b, d, j: (0, 0)),
            pl.BlockSpec((None, 1, LANES), lambda b, d, j: (d, 0, 0)),
            pl.BlockSpec((None, 1, LANES), lambda b, d, j: (d, 0, 0)),
            pl.BlockSpec((None, 1, mix), lambda b, d, j: (d, 0, 0)),
            pl.BlockSpec((1, mix), lambda b, d, j: (0, 0)),
            pl.BlockSpec((3 * LANES, mix), lambda b, d, j: (0, 0)),
        ],
        out_specs=pl.BlockSpec((None, ROW_BLOCK, mix),
                               lambda b, d, j: (b, jnp.where(d == 0, 0, blk_map(b, d, j)), 0)),
        out_shape=jax.ShapeDtypeStruct((nb, r, mix), BF16),
        scratch_shapes=[
            pltpu.VMEM((ROW_BLOCK + 2 * SSD_HALO, cdim), F32),
            pltpu.VMEM((r, mix), F32),
            pltpu.VMEM((r, bcw), BF16),
            pltpu.VMEM((r, bcw), BF16),
            pltpu.VMEM((r // SSD_CHUNK, bcw, SSD_CHUNK), BF16),
            pltpu.VMEM((r, mix), F32),
            pltpu.VMEM((ROW_BLOCK, mix), F32),
            pltpu.VMEM((SSD_STATE, mix), F32),
        ],
        compiler_params=_params("arbitrary", "arbitrary", "arbitrary"),
        name="ssd",
    )(p3, p3, p3, p3, p3, conv_w, conv_b.reshape(1, cdim), dtb, alog, dsk,
      norm_g.reshape(1, mix), e3)


def _gmlp_kernel(p_ref, g_ref, ws_ref, bias_ref, o_ref, *, mix):
    p = p_ref[...]
    ge = 0.5 * p * (1.0 + jnp.tanh(math.sqrt(2.0 / math.pi) * (p + 0.044715 * (p * p * p))))
    u = ge[:, :mix]
    v = _rms(ge[:, mix:], g_ref[...]).astype(BF16)
    gw = mix // GMLP_GROUPS
    for c in range(ROW_BLOCK // GMLP_CHUNK):
        r0 = c * GMLP_CHUNK
        outs = []
        for g in range(GMLP_GROUPS):
            outs.append(_dot(ws_ref[g], v[r0:r0 + GMLP_CHUNK, g * gw:(g + 1) * gw]))
        mixed = jnp.concatenate(outs, axis=1) + bias_ref[...]
        o_ref[r0:r0 + GMLP_CHUNK, :] = (u[r0:r0 + GMLP_CHUNK, :] * mixed).astype(o_ref.dtype)


def _gmlp(p3, norm_g, ws, bs, lay, *, mix):
    nb, r, _ = p3.shape
    nblk = r // ROW_BLOCK
    bias_full = jnp.repeat(bs.T, mix // GMLP_GROUPS, axis=1)
    gm_c = lay["gm"]
    return pl.pallas_call(
        functools.partial(_gmlp_kernel, mix=mix),
        grid=(nb, nblk),
        in_specs=[
            pl.BlockSpec((None, ROW_BLOCK, 2 * mix), lambda b, j: (b, j, gm_c)),
            pl.BlockSpec((1, mix), lambda b, j: (0, 0)),
            pl.BlockSpec((GMLP_GROUPS, GMLP_CHUNK, GMLP_CHUNK), lambda b, j: (0, 0, 0)),
            pl.BlockSpec((GMLP_CHUNK, mix), lambda b, j: (0, 0)),
        ],
        out_specs=pl.BlockSpec((None, ROW_BLOCK, mix), lambda b, j: (b, j, 0)),
        out_shape=jax.ShapeDtypeStruct((nb, r, mix), BF16),
        compiler_params=_params("arbitrary", "arbitrary"),
        name="gmlp",
    )(p3, norm_g.reshape(1, mix), ws.astype(BF16), bias_full)


def _conf_kernel(prev_ref, cur_ref, next_ref, w_ref, b_ref, lg_ref, lb_ref, o_ref, gw_ref, ph_ref,
                 *, nblk, mix):
    j = pl.program_id(1)

    def glu(v):
        return v[:, :mix] * jax.nn.sigmoid(v[:, mix:])

    has_prev = jnp.logical_and(j != 0, j != 1)
    has_next = jnp.logical_and(j != 0, j != nblk - 1)
    gw_ref[0:CONF_HALO, :] = jnp.where(has_prev, glu(prev_ref[...]), 0.0)
    gw_ref[CONF_HALO:CONF_HALO + ROW_BLOCK, :] = glu(cur_ref[...])
    gw_ref[CONF_HALO + ROW_BLOCK:, :] = jnp.where(has_next, glu(next_ref[...]), 0.0)
    span = ph_ref.shape[1]
    for r in range(1, SUBLANES):
        ph_ref[r - 1] = gw_ref[pl.ds(r, span), :]
    pad = (CONV_KERNEL - 1) // 2
    acc = jnp.broadcast_to(b_ref[...], (ROW_BLOCK, mix))
    for k in range(CONV_KERNEL):
        a, r = divmod(CONF_HALO - pad + k, SUBLANES)
        if r == 0:
            win = gw_ref[pl.ds(a * SUBLANES, ROW_BLOCK), :]
        else:
            win = ph_ref[r - 1, pl.ds(a * SUBLANES, ROW_BLOCK), :]
        acc = acc + w_ref[k:k + 1, :] * win
    mu = jnp.mean(acc, axis=-1, keepdims=True)
    cen = acc - mu
    var = jnp.mean(cen * cen, axis=-1, keepdims=True)
    y = cen * lax.rsqrt(var + EPS) * lg_ref[...] + lb_ref[...]
    o_ref[...] = _silu(y).astype(o_ref.dtype)


def _conf(p3, dw_w, dw_b, ln_g, ln_b, lay, *, mix):
    nb, r, _ = p3.shape
    nblk = r // ROW_BLOCK
    hb = ROW_BLOCK // CONF_HALO
    cv_c = lay["cv"]
    win_rows = ROW_BLOCK + 2 * CONF_HALO
    return pl.pallas_call(
        functools.partial(_conf_kernel, nblk=nblk, mix=mix),
        grid=(nb, nblk),
        in_specs=[
            pl.BlockSpec((None, CONF_HALO, 2 * mix), lambda b, j: (b, jnp.maximum(j * hb - 1, 0), cv_c)),
            pl.BlockSpec((None, ROW_BLOCK, 2 * mix), lambda b, j: (b, j, cv_c)),
            pl.BlockSpec((None, CONF_HALO, 2 * mix),
                         lambda b, j: (b, jnp.minimum((j + 1) * hb, nblk * hb - 1), cv_c)),
            pl.BlockSpec((CONV_KERNEL, mix), lambda b, j: (0, 0)),
            pl.BlockSpec((1, mix), lambda b, j: (0, 0)),
            pl.BlockSpec((1, mix), lambda b, j: (0, 0)),
            pl.BlockSpec((1, mix), lambda b, j: (0, 0)),
        ],
        out_specs=pl.BlockSpec((None, ROW_BLOCK, mix), lambda b, j: (b, j, 0)),
        out_shape=jax.ShapeDtypeStruct((nb, r, mix), BF16),
        scratch_shapes=[pltpu.VMEM((win_rows, mix), F32),
                        pltpu.VMEM((SUBLANES - 1, win_rows - SUBLANES, mix), F32)],
        compiler_params=_params("arbitrary", "arbitrary"),
        name="conf",
    )(p3, p3, p3, dw_w, dw_b.reshape(1, mix), ln_g.reshape(1, mix), ln_b.reshape(1, mix))


def _rope(x, cos, sin_signed):
    lane = lax.broadcasted_iota(jnp.int32, x.shape, 1)
    first = (lane & (HEAD_DIM // 4)) == 0
    partner = jnp.where(first, pltpu.roll(x, HEAD_DIM - HEAD_DIM // 4, 1), pltpu.roll(x, HEAD_DIM // 4, 1))
    return x * cos + partner * sin_signed


def _lane_blocks(x):
    return [x[:, c:c + LANES] for c in range(0, x.shape[1], LANES)]


def _attn_kernel(sink_ref, q_ref, k_ref, v_ref, cos_ref, sin_ref, o_ref, kr_ref, vb_ref, *,
                 ctx_len, n_kv, rows):
    j = pl.program_id(1)
    blk = WINDOW
    scale = HEAD_DIM ** -0.5

    @pl.when(j == 0)
    def _():
        for g in range(n_kv):
            kr_ref[:, g * HEAD_DIM:(g + 1) * HEAD_DIM] = _rope(
                k_ref[:, g * HEAD_DIM:(g + 1) * HEAD_DIM], cos_ref[...], sin_ref[...]).astype(BF16)
        vb_ref[...] = v_ref[...].astype(BF16)

    q0 = pl.multiple_of(j * blk, blk)
    cos_q = cos_ref[pl.ds(q0, blk), :]
    sin_q = sin_ref[pl.ds(q0, blk), :]
    band0 = pl.multiple_of(jnp.clip((j - 1) * blk, 0, rows - 3 * blk), blk)
    qpos = q0 + lax.broadcasted_iota(jnp.int32, (blk, 3 * blk), 0)
    kpos = band0 + lax.broadcasted_iota(jnp.int32, (blk, 3 * blk), 1)
    ok = jnp.logical_and(jnp.logical_and(qpos >= ctx_len, kpos >= ctx_len), jnp.abs(kpos - qpos) <= WINDOW)
    bias = jnp.where(ok, 0.0, -jnp.inf)
    bias = jnp.concatenate([bias] * KV_RATIO, axis=0)
    hrow = lax.broadcasted_iota(jnp.int32, (KV_RATIO * blk, 1), 0) // blk

    for g in range(n_kv):
        gc = slice(g * HEAD_DIM, (g + 1) * HEAD_DIM)
        k_band = kr_ref[pl.ds(band0, 3 * blk), gc]
        v_band = vb_ref[pl.ds(band0, 3 * blk), gc]
        k_ctx = kr_ref[0:ctx_len, gc]
        v_ctx = vb_ref[0:ctx_len, gc]
        qs = []
        sink = jnp.zeros((KV_RATIO * blk, 1), F32)
        for r in range(KV_RATIO):
            h = g * KV_RATIO + r
            qs.append(_rope(q_ref[:, h * HEAD_DIM:(h + 1) * HEAD_DIM], cos_q, sin_q).astype(BF16))
            sink = jnp.where(hrow == r, sink_ref[h], sink)
        qg = jnp.concatenate(qs, axis=0)
        s_band = _dot_nt(qg, k_band) * scale + bias
        s_ctx = _dot_nt(qg, k_ctx) * scale
        mx = functools.reduce(jnp.maximum, _lane_blocks(s_band) + _lane_blocks(s_ctx))
        m = jnp.maximum(jnp.max(mx, axis=-1, keepdims=True), sink)
        p_band = jnp.exp(s_band - m)
        p_ctx = jnp.exp(s_ctx - m)
        psum = functools.reduce(lambda u, w: u + w, _lane_blocks(p_band) + _lane_blocks(p_ctx))
        den = jnp.sum(psum, axis=-1, keepdims=True) + jnp.exp(sink - m)
        o = (_dot(p_band.astype(BF16), v_band) + _dot(p_ctx.astype(BF16), v_ctx)) / den
        for r in range(KV_RATIO):
            h = g * KV_RATIO + r
            o_ref[:, h * HEAD_DIM:(h + 1) * HEAD_DIM] = o[r * blk:(r + 1) * blk, :].astype(o_ref.dtype)


def _attn(p3, sink, cos_t, sin_t, lay, *, mix, ctx_len):
    nb, r, _ = p3.shape
    n_heads = mix // HEAD_DIM
    n_kv = n_heads // KV_RATIO
    kvw = n_kv * HEAD_DIM
    q_c, k_c, v_c = lay["q"], lay["k"], lay["v"]
    kern = functools.partial(_attn_kernel, ctx_len=ctx_len, n_kv=n_kv, rows=r)
    return pl.pallas_call(
        kern,
        grid=(nb, r // WINDOW),
        in_specs=[
            pl.BlockSpec(memory_space=pltpu.SMEM),
            pl.BlockSpec((None, WINDOW, mix), lambda b, j: (b, j, q_c)),
            pl.BlockSpec((None, r, kvw), lambda b, j: (b, 0, k_c)),
            pl.BlockSpec((None, r, kvw), lambda b, j: (b, 0, v_c)),
            pl.BlockSpec((r, HEAD_DIM), lambda b, j: (0, 0)),
            pl.BlockSpec((r, HEAD_DIM), lambda b, j: (0, 0)),
        ],
        out_specs=pl.BlockSpec((None, WINDOW, mix), lambda b, j: (b, j, 0)),
        out_shape=jax.ShapeDtypeStruct((nb, r, mix), BF16),
        scratch_shapes=[pltpu.VMEM((r, kvw), BF16), pltpu.VMEM((r, kvw), BF16)],
        compiler_params=_params("arbitrary", "arbitrary"),
        name="attn",
    )(sink, p3, p3, p3, cos_t, sin_t)


def _rope_tables(seq, ctx_len):
    quarter = HEAD_DIM // 4
    t = jnp.arange(seq, dtype=jnp.int32)
    rows = (t // GRID_W).astype(F32)
    cols = (t % GRID_W).astype(F32)
    freqs = ROPE_BASE ** (-jnp.arange(quarter, dtype=F32) / quarter)
    ang_r = rows[:, None] * freqs
    ang_c = cols[:, None] * freqs
    cos = jnp.concatenate([jnp.cos(ang_r)] * 2 + [jnp.cos(ang_c)] * 2, axis=1)
    sin = jnp.concatenate([-jnp.sin(ang_r), jnp.sin(ang_r), -jnp.sin(ang_c), jnp.sin(ang_c)], axis=1)
    cos = jnp.concatenate([jnp.ones((ctx_len, HEAD_DIM), F32), cos], axis=0)
    sin = jnp.concatenate([jnp.zeros((ctx_len, HEAD_DIM), F32), sin], axis=0)
    return cos, sin


def _merge_kernel(h_ref, ya_ref, yb_ref, yc_ref, yd_ref, wg_ref, bg_ref, wb_ref, o_ref):
    h = h_ref[...]
    acc = None
    for b, y_ref in enumerate((ya_ref, yb_ref, yc_ref, yd_ref)):
        gate = jax.nn.sigmoid(_dot(h, wg_ref[b]) + bg_ref[b])
        term = gate * _dot(y_ref[...], wb_ref[b])
        acc = term if acc is None else acc + term
    o_ref[...] = acc.astype(o_ref.dtype)


def _merge(h2, ys, w_gate, b_gate, w_branch, layer, *, tm):
    n, d = h2.shape
    mix = w_branch.shape[2]
    tn = MERGE_TN
    y_spec = pl.BlockSpec((tm, mix), lambda i, j: (i, 0))
    return pl.pallas_call(
        _merge_kernel,
        grid=(n // tm, d // tn),
        in_specs=[
            pl.BlockSpec((tm, d), lambda i, j: (i, 0)),
            y_spec, y_spec, y_spec, y_spec,
            pl.BlockSpec((None, N_BRANCH, d, tn), lambda i, j: (layer, 0, 0, j)),
            pl.BlockSpec((N_BRANCH, 1, tn), lambda i, j: (0, 0, j)),
            pl.BlockSpec((None, N_BRANCH, mix, tn), lambda i, j: (layer, 0, 0, j)),
        ],
        out_specs=pl.BlockSpec((tm, tn), lambda i, j: (i, j)),
        out_shape=jax.ShapeDtypeStruct((n, d), BF16),
        compiler_params=_params("arbitrary", "arbitrary"),
        name="merge",
    )(h2, *ys, w_gate, b_gate.reshape(N_BRANCH, 1, d), w_branch)


def _outproj_kernel(x_ref, ga_b, ga_c, m_ref, w_ref, o_ref, *, tiles_per_batch, ctx_len):
    i = pl.program_id(0)
    tm, d = x_ref.shape
    m = m_ref[...]
    for c in range(0, d, OUT_TN):
        y = _dot(m, w_ref[:, c:c + OUT_TN])
        for rows, gate in _row_parts(i, tiles_per_batch, ctx_len, tm, ga_c, ga_b):
            o_ref[rows, c:c + OUT_TN] = x_ref[rows, c:c + OUT_TN] + gate[:, c:c + OUT_TN] * y[rows, :]


def _outproj(x2, mods, merged, w_out, layer, *, tm, tiles_per_batch, n_batch, ctx_len):
    n, d = x2.shape
    _, _, gate = mods
    bs, cs = _mod_specs(tiles_per_batch, n_batch, d)
    kern = functools.partial(_outproj_kernel, tiles_per_batch=tiles_per_batch, ctx_len=ctx_len)
    return pl.pallas_call(
        kern,
        grid=(n // tm,),
        in_specs=[
            pl.BlockSpec((tm, d), lambda i: (i, 0)),
            bs, cs,
            pl.BlockSpec((tm, d), lambda i: (i, 0)),
            pl.BlockSpec((None, d, d), lambda i: (layer, 0, 0), pipeline_mode=pl.Buffered(1)),
        ],
        out_specs=pl.BlockSpec((tm, d), lambda i: (i, 0)),
        out_shape=jax.ShapeDtypeStruct((n, d), F32),
        compiler_params=_params("arbitrary"),
        name="outproj",
    )(x2, gate, gate, merged, w_out)


def _final_kernel(x_ref, g_ref, o_ref):
    o_ref[...] = _rms(x_ref[...], g_ref[...])


def _final(x3, g, *, ctx_len, seq):
    nb, _, d = x3.shape
    off = ctx_len // ROW_BLOCK
    return pl.pallas_call(
        _final_kernel,
        grid=(nb, seq // ROW_BLOCK),
        in_specs=[
            pl.BlockSpec((None, ROW_BLOCK, d), lambda b, j: (b, j + off, 0)),
            pl.BlockSpec((1, d), lambda b, j: (0, 0)),
        ],
        out_specs=pl.BlockSpec((None, ROW_BLOCK, d), lambda b, j: (b, j, 0)),
        out_shape=jax.ShapeDtypeStruct((nb, seq, d), F32),
        compiler_params=_params("arbitrary", "arbitrary"),
        name="final",
    )(x3, g.reshape(1, d))


def _proj_layout(mix, n_kv):
    cdim = mix + 2 * SSD_GROUPS * SSD_STATE
    heads = mix // SSD_HEADDIM
    kvw = n_kv * HEAD_DIM
    sizes = dict(z=mix, xbc=cdim, dt=heads, gm=2 * mix, cv=2 * mix, q=mix, k=kvw, v=kvw)
    src, o = {}, 0
    for name in ("z", "xbc", "dt", "gm", "cv", "q", "k", "v"):
        src[name] = (o, sizes[name])
        o += sizes[name]
    order = ("xbc", "gm", "cv", "z", "q", "k", "v", "dt")
    dst, o = {}, 0
    for name in order:
        width = LANES if name == "dt" else sizes[name]
        assert o % width == 0
        dst[name] = o // width
        o += width
    total = -(-o // INPROJ_TN) * INPROJ_TN
    return src, order, dst, total


def _permute_w_in(w_in, src, order, total):
    cols = []
    for name in order:
        s, w = src[name]
        cols.append(w_in[:, :, s:s + w].astype(BF16))
    out = jnp.concatenate(cols, axis=2)
    return jnp.pad(out, ((0, 0), (0, 0), (0, total - out.shape[2])))


def kernel(x, c, ctx, c_ctx, w_ada, b_ada, ffn1_norm, ffn1_w13, ffn1_w2, mix_norm, w_in, w_gate, b_gate, w_branch, w_out, ssd_conv_w, ssd_conv_b, ssd_dt_bias, ssd_a_log, ssd_d, ssd_norm, gmlp_norm, gmlp_ws, gmlp_bs, conv_dw_w, conv_dw_b, conv_ln_g, conv_ln_b, attn_sink, ffn2_norm, ffn2_w13, ffn2_w2, final_norm):
    nb, seq, d = x.shape
    ctx_len = ctx.shape[1]
    depth = w_ada.shape[0]
    mix = w_branch.shape[2]
    n_kv = mix // HEAD_DIM // KV_RATIO
    r = ctx_len + seq
    assert ctx_len == ROW_BLOCK and seq % ROW_BLOCK == 0 and nb + 1 <= MOD_ROWS
    tm = TOKEN_TILE
    assert r % tm == 0
    tiles_per_batch = r // tm
    tile_kw = dict(tm=tm, tiles_per_batch=tiles_per_batch, n_batch=nb, ctx_len=ctx_len)

    src, order, dst, total = _proj_layout(mix, n_kv)
    cos_t, sin_t = _rope_tables(seq, ctx_len)

    ffn1_w13_b, ffn1_w2_b = ffn1_w13.astype(BF16), ffn1_w2.astype(BF16)
    ffn2_w13_b, ffn2_w2_b = ffn2_w13.astype(BF16), ffn2_w2.astype(BF16)
    w_in_b = _permute_w_in(w_in, src, order, total)
    w_gate_b, w_branch_b, w_out_b = w_gate.astype(BF16), w_branch.astype(BF16), w_out.astype(BF16)

    c_all = jnp.concatenate([c, c_ctx[None, :], jnp.zeros((MOD_ROWS - nb - 1, d), F32)], axis=0)
    mod_all = _ada(c_all, w_ada, b_ada).reshape(depth, MOD_ROWS, N_MOD, 1, d)

    xcur = jnp.concatenate([ctx, x], axis=1).reshape(nb * r, d)
    for l in range(depth):
        def mods(k, l=l):
            return tuple(mod_all[l, :, 3 * k + t] for t in range(3))

        xcur = _ffn(xcur, mods(0), ffn1_norm[l], ffn1_w13_b, ffn1_w2_b, l, **tile_kw)
        p2, h2 = _inproj(xcur, mods(1), mix_norm[l], w_in_b, l, **tile_kw)
        p3 = p2.reshape(nb, r, total)
        ya = _ssd(p3, ssd_conv_w[l], ssd_conv_b[l], ssd_dt_bias[l], ssd_a_log[l], ssd_d[l],
                  ssd_norm[l], dst, mix=mix)
        yb = _gmlp(p3, gmlp_norm[l], gmlp_ws[l], gmlp_bs[l], dst, mix=mix)
        yc = _conf(p3, conv_dw_w[l], conv_dw_b[l], conv_ln_g[l], conv_ln_b[l], dst, mix=mix)
        yd = _attn(p3, attn_sink[l], cos_t, sin_t, dst, mix=mix, ctx_len=ctx_len)
        ys = tuple(y.reshape(nb * r, mix) for y in (ya, yb, yc, yd))
        merged = _merge(h2, ys, w_gate_b, b_gate[l], w_branch_b, l, tm=tm)
        xcur = _outproj(xcur, mods(1), merged, w_out_b, l, **tile_kw)
        xcur = _ffn(xcur, mods(2), ffn2_norm[l], ffn2_w13_b, ffn2_w2_b, l, **tile_kw)
    return _final(xcur.reshape(nb, r, d), final_norm, ctx_len=ctx_len, seq=seq)
```

```python
import functools
import math

import jax
import jax.numpy as jnp
from jax import lax
from jax.experimental import pallas as pl
from jax.experimental.pallas import tpu as pltpu

F32 = jnp.float32
BF16 = jnp.bfloat16

GRID_W = 64
N_BRANCH = 4
N_MOD = 9
HALF_STEP = 0.5
SSD_HEADDIM = 64
SSD_GROUPS = 4
SSD_STATE = 128
SSD_CONV = 5
SSD_CHUNK = 128
GMLP_GROUPS = 8
GMLP_CHUNK = 128
CONV_KERNEL = 31
HEAD_DIM = 128
KV_RATIO = 4
WINDOW = 128
ROPE_BASE = 10000.0
EPS = 1e-6

LANES = 128
SUBLANES = 8
VMEM_LIMIT_BYTES = 56 * 1024 * 1024

TOKEN_TILE = 768
FFN_TF = 512
INPROJ_TN = 1536
MERGE_TN = 512
OUT_TN = 512
ADA_TN = 1024
ROW_BLOCK = 256
SSD_HALO = 8
CONF_HALO = 16
MOD_ROWS = 24


def _params(*sem):
    return pltpu.CompilerParams(dimension_semantics=sem, vmem_limit_bytes=VMEM_LIMIT_BYTES)


def _silu(x):
    return x * jax.nn.sigmoid(x)


def _dot(a, b):
    return jnp.dot(a, b, preferred_element_type=F32)


def _dot_nt(a, b):
    return lax.dot_general(a, b, (((1,), (1,)), ((), ())), preferred_element_type=F32)


def _rms(x, g):
    return x * lax.rsqrt(jnp.mean(x * x, axis=-1, keepdims=True) + EPS) * g


def _split3(v):
    hi = v.astype(BF16)
    r1 = v - hi.astype(F32)
    mid = r1.astype(BF16)
    lo = (r1 - mid.astype(F32)).astype(BF16)
    return hi, mid, lo


def _exact_dot_left(m01, v):
    hi, mid, lo = _split3(v)
    return _dot(m01, hi) + _dot(m01, mid) + _dot(m01, lo)


def _row_parts(tile, tiles_per_batch, ctx_len, tm, ctx_ref, batch_ref):
    first = tile % tiles_per_batch == 0
    top = jnp.where(first, ctx_ref[...], batch_ref[...])
    return ((slice(0, ctx_len), top), (slice(ctx_len, tm), batch_ref[...]))


def _modnorm_groups(h_ref, x_ref, g_ref, tile, tiles_per_batch, ctx_len, sh_b, sh_c, sc_b, sc_c):
    tm = x_ref.shape[0]
    first = tile % tiles_per_batch == 0
    for r0 in range(0, tm, ctx_len):
        rows = slice(r0, r0 + ctx_len)
        if r0 == 0:
            shift = jnp.where(first, sh_c[...], sh_b[...])
            scale = jnp.where(first, sc_c[...], sc_b[...])
        else:
            shift, scale = sh_b[...], sc_b[...]
        h_ref[rows, :] = (_rms(x_ref[rows, :], g_ref[...]) * (1.0 + scale) + shift).astype(h_ref.dtype)
        yield rows


def _ada_kernel(c_ref, w_ref, b_ref, o_ref):
    a = _silu(c_ref[...]).astype(BF16)
    o_ref[...] = _dot(a, w_ref[...].astype(BF16)) + b_ref[...]


def _ada(c_all, w_ada, b_ada):
    depth, d, nm = w_ada.shape
    tn = ADA_TN
    return pl.pallas_call(
        _ada_kernel,
        grid=(depth, nm // tn),
        in_specs=[
            pl.BlockSpec((MOD_ROWS, d), lambda l, n: (0, 0)),
            pl.BlockSpec((None, d, tn), lambda l, n: (l, 0, n)),
            pl.BlockSpec((None, 1, tn), lambda l, n: (l, 0, n)),
        ],
        out_specs=pl.BlockSpec((None, MOD_ROWS, tn), lambda l, n: (l, 0, n)),
        out_shape=jax.ShapeDtypeStruct((depth, MOD_ROWS, nm), F32),
        compiler_params=_params("arbitrary", "arbitrary"),
        name="ada",
    )(c_all, w_ada, b_ada.reshape(depth, 1, nm))


def _mod_specs(tiles_per_batch, n_batch, d):
    def batch_map(i, *_):
        return (i // tiles_per_batch, 0, 0)

    def ctx_map(i, *_):
        return (n_batch, 0, 0)

    return (pl.BlockSpec((None, 1, d), batch_map), pl.BlockSpec((None, 1, d), ctx_map))


def _ffn_kernel(x_ref, sh_b, sh_c, sc_b, sc_c, ga_b, ga_c, g_ref, w1_ref, w3_ref, w2_ref,
                o_ref, h_ref, *, tiles_per_batch, ctx_len, nf):
    i = pl.program_id(0)
    f = pl.program_id(1)
    tm = x_ref.shape[0]

    def swiglu(h):
        a = _dot(h, w1_ref[...])
        b = _dot(h, w3_ref[...])
        return _dot((_silu(a) * b).astype(BF16), w2_ref[...])

    @pl.when(f == 0)
    def _():
        for rows in _modnorm_groups(h_ref, x_ref, g_ref, i, tiles_per_batch, ctx_len,
                                    sh_b, sh_c, sc_b, sc_c):
            o_ref[rows, :] = swiglu(h_ref[rows, :])

    @pl.when(f > 0)
    def _():
        o_ref[...] += swiglu(h_ref[...])

    @pl.when(f == nf - 1)
    def _():
        for rows, gate in _row_parts(i, tiles_per_batch, ctx_len, tm, ga_c, ga_b):
            o_ref[rows, :] = x_ref[rows, :] + HALF_STEP * gate * o_ref[rows, :]


def _ffn(x2, mods, g, w13, w2, layer, *, tm, tiles_per_batch, n_batch, ctx_len):
    n, d = x2.shape
    fdim = w2.shape[1]
    tf = FFN_TF
    nf = fdim // tf
    shift, scale, gate = mods
    bs, cs = _mod_specs(tiles_per_batch, n_batch, d)
    kern = functools.partial(_ffn_kernel, tiles_per_batch=tiles_per_batch, ctx_len=ctx_len, nf=nf)
    return pl.pallas_call(
        kern,
        grid=(n // tm, nf),
        in_specs=[
            pl.BlockSpec((tm, d), lambda i, f: (i, 0)),
            bs, cs, bs, cs, bs, cs,
            pl.BlockSpec((1, d), lambda i, f: (0, 0)),
            pl.BlockSpec((None, d, tf), lambda i, f: (layer, 0, f)),
            pl.BlockSpec((None, d, tf), lambda i, f: (layer, 0, f + nf)),
            pl.BlockSpec((None, tf, d), lambda i, f: (layer, f, 0)),
        ],
        out_specs=pl.BlockSpec((tm, d), lambda i, f: (i, 0)),
        out_shape=jax.ShapeDtypeStruct((n, d), F32),
        scratch_shapes=[pltpu.VMEM((tm, d), BF16)],
        compiler_params=_params("arbitrary", "arbitrary"),
        name="ffn",
    )(x2, shift, shift, scale, scale, gate, gate, g.reshape(1, d), w13, w13, w2)


def _inproj_kernel(x_ref, sh_b, sh_c, sc_b, sc_c, g_ref, w_ref, o_ref, h_ref, *,
                   tiles_per_batch, ctx_len):
    i = pl.program_id(0)
    n = pl.program_id(1)

    @pl.when(n == 0)
    def _():
        for rows in _modnorm_groups(h_ref, x_ref, g_ref, i, tiles_per_batch, ctx_len,
                                    sh_b, sh_c, sc_b, sc_c):
            o_ref[rows, :] = _dot(h_ref[rows, :], w_ref[...])

    @pl.when(n > 0)
    def _():
        o_ref[...] = _dot(h_ref[...], w_ref[...])


def _inproj(x2, mods, g, w, layer, *, tm, tiles_per_batch, n_batch, ctx_len):
    n, d = x2.shape
    width = w.shape[2]
    tn = INPROJ_TN
    shift, scale, _ = mods
    bs, cs = _mod_specs(tiles_per_batch, n_batch, d)
    kern = functools.partial(_inproj_kernel, tiles_per_batch=tiles_per_batch, ctx_len=ctx_len)
    return pl.pallas_call(
        kern,
        grid=(n // tm, width // tn),
        in_specs=[
            pl.BlockSpec((tm, d), lambda i, j: (i, 0)),
            bs, cs, bs, cs,
            pl.BlockSpec((1, d), lambda i, j: (0, 0)),
            pl.BlockSpec((None, d, tn), lambda i, j: (layer, 0, j)),
        ],
        out_specs=[pl.BlockSpec((tm, tn), lambda i, j: (i, j)),
                   pl.BlockSpec((tm, d), lambda i, j: (i, 0))],
        out_shape=[jax.ShapeDtypeStruct((n, width), F32), jax.ShapeDtypeStruct((n, d), BF16)],
        compiler_params=_params("arbitrary", "arbitrary"),
        name="inproj",
    )(x2, shift, shift, scale, scale, g.reshape(1, d), w)


def _ssd_block_of(d, j, nblk):
    return jnp.where(d == 0, j, jnp.where(j == 0, 0, nblk - j))


def _ssd_kernel(prev_ref, cur_ref, next_ref, z_ref, dt_ref, cw_ref, cb_ref, dtb_ref, alog_ref,
                dsk_ref, ng_ref, e3_ref, o_ref, xw_ref, x_ref, b_ref, c_ref, bt_ref, yf_ref, yc_ref, st_ref,
                *, nblk, mix):
    d = pl.program_id(1)
    j = pl.program_id(2)
    blk = _ssd_block_of(d, j, nblk)
    q = SSD_CHUNK
    gw = SSD_STATE
    bcw = SSD_GROUPS * gw
    hpg = mix // SSD_HEADDIM // SSD_GROUPS
    nch = ROW_BLOCK // q
    row0 = pl.multiple_of(blk * ROW_BLOCK, ROW_BLOCK)

    @pl.when(d == 0)
    def _():
        has_prev = jnp.logical_and(blk != 0, blk != 1)
        has_next = jnp.logical_and(blk != 0, blk != nblk - 1)
        xw_ref[0:SSD_HALO, :] = jnp.where(has_prev, prev_ref[...], 0.0)
        xw_ref[SSD_HALO:SSD_HALO + ROW_BLOCK, :] = cur_ref[...]
        xw_ref[SSD_HALO + ROW_BLOCK:, :] = jnp.where(has_next, next_ref[...], 0.0)
        pad = (SSD_CONV - 1) // 2
        acc = jnp.broadcast_to(cb_ref[...], (ROW_BLOCK, cb_ref.shape[1]))
        for k in range(SSD_CONV):
            acc = acc + cw_ref[k:k + 1, :] * xw_ref[pl.ds(SSD_HALO - pad + k, ROW_BLOCK), :]
        hs = _silu(acc)
        x_ref[pl.ds(row0, ROW_BLOCK), :] = hs[:, 0:mix]
        b_ref[pl.ds(row0, ROW_BLOCK), :] = hs[:, mix:mix + bcw].astype(BF16)
        c_ref[pl.ds(row0, ROW_BLOCK), :] = hs[:, mix + bcw:mix + 2 * bcw].astype(BF16)
        for ci in range(nch):
            for g in range(SSD_GROUPS):
                bt_ref[blk * nch + ci, g * gw:(g + 1) * gw, :] = (
                    hs[ci * q:(ci + 1) * q, mix + g * gw:mix + (g + 1) * gw].T)

    @pl.when(j == 0)
    def _():
        st_ref[...] = jnp.zeros_like(st_ref)

    ii = lax.broadcasted_iota(jnp.int32, (q, q), 0)
    jj = lax.broadcasted_iota(jnp.int32, (q, q), 1)
    sgn = 1 - 2 * d
    tri = (ii - jj) * sgn >= 0
    tri_b = jnp.where(tri, 1.0, 0.0).astype(BF16)
    lane = lax.broadcasted_iota(jnp.int32, (q, LANES), 1)
    lo_half = lane < SSD_HEADDIM

    a_neg = -jnp.exp(alog_ref[...])
    e3 = e3_ref[...]

    for ci in range(nch):
        chunk = jnp.where(d == 0, ci, nch - 1 - ci)
        r0 = pl.multiple_of(chunk * q, q)
        grow = pl.multiple_of(row0 + r0, q)
        x = x_ref[pl.ds(grow, q), :]
        dt_raw = dt_ref[pl.ds(r0, q), :] + dtb_ref[...]
        dt = jnp.maximum(dt_raw, 0.0) + jnp.log1p(jnp.exp(-jnp.abs(dt_raw)))
        a = dt * a_neg
        cum = _exact_dot_left(tri_b, a)
        total = jnp.sum(a, axis=0, keepdims=True)
        cum_t = cum.T
        ecum = jnp.exp(cum)
        dt_t = dt.T
        s_t = (dt * jnp.exp(total - cum)).T
        v = jnp.concatenate([ecum, jnp.broadcast_to(jnp.exp(total), (SUBLANES, LANES))], axis=0)
        ex = _dot(jnp.concatenate(_split3(v), axis=1), e3)
        ec_x, cd_x = ex[0:q], ex[q:q + 1]
        ys = []
        for g in range(SSD_GROUPS):
            bg_b = b_ref[pl.ds(grow, q), g * gw:(g + 1) * gw]
            cg_b = c_ref[pl.ds(grow, q), g * gw:(g + 1) * gw]
            btg = bt_ref[blk * nch + chunk, g * gw:(g + 1) * gw, :]
            cb = _dot_nt(cg_b, bg_b)
            c0 = g * hpg * SSD_HEADDIM
            cw = hpg * SSD_HEADDIM
            y_off = _dot(cg_b, st_ref[:, c0:c0 + cw].astype(BF16)) * ec_x[:, c0:c0 + cw]
            y_pairs = []
            for pr in range(hpg // 2):
                l0 = c0 + pr * LANES
                xp = x[:, l0:l0 + LANES]
                ms, bts, xms = [], [], []
                for half in range(2):
                    h = g * hpg + pr * 2 + half
                    seg = cum[:, h:h + 1] - cum_t[h:h + 1, :]
                    dec = jnp.exp(jnp.where(tri, seg, -jnp.inf))
                    keep = lo_half if half == 0 else jnp.logical_not(lo_half)
                    ms.append((cb * dec * dt_t[h:h + 1, :]).astype(BF16))
                    bts.append((btg * s_t[h:h + 1, :]).astype(BF16))
                    xms.append(jnp.where(keep, xp, 0.0).astype(BF16))
                xm = jnp.concatenate(xms, axis=0)
                y_pairs.append(_dot(jnp.concatenate(ms, axis=1), xm))
                st_new = _dot(jnp.concatenate(bts, axis=1), xm)
                st_ref[:, l0:l0 + LANES] = st_ref[:, l0:l0 + LANES] * cd_x[:, l0:l0 + LANES] + st_new
            ys.append(jnp.concatenate(y_pairs, axis=1) + y_off)
        yc_ref[pl.ds(r0, q), :] = jnp.concatenate(ys, axis=1) + x * dsk_ref[...]

    @pl.when(d == 0)
    def _():
        yf_ref[pl.ds(row0, ROW_BLOCK), :] = yc_ref[...]

    @pl.when(d == 1)
    def _():
        yt = (yf_ref[pl.ds(row0, ROW_BLOCK), :] + yc_ref[...]) * _silu(z_ref[...])
        o_ref[...] = _rms(yt, ng_ref[...]).astype(o_ref.dtype)


def _ssd(p3, conv_w, conv_b, dt_bias, a_log, d_skip, norm_g, lay, *, mix):
    nb, r, _ = p3.shape
    nblk = r // ROW_BLOCK
    cdim = conv_w.shape[1]
    heads = mix // SSD_HEADDIM
    hb = ROW_BLOCK // SSD_HALO
    bcw = SSD_GROUPS * SSD_STATE

    def blk_map(b, d, j):
        return _ssd_block_of(d, j, nblk)

    def conv_blk(b, d, j):
        return jnp.where(d == 0, j, nblk - 1)

    pad_h = LANES - heads
    dtb = jnp.pad(dt_bias, ((0, 0), (0, pad_h))).reshape(2, 1, LANES)
    alog = jnp.pad(a_log, ((0, 0), (0, pad_h))).reshape(2, 1, LANES)
    dsk = jnp.repeat(d_skip, SSD_HEADDIM, axis=1).reshape(2, 1, mix)
    e01 = (jnp.arange(LANES)[:, None] == (jnp.arange(mix)[None, :] // SSD_HEADDIM)).astype(BF16)
    e3 = jnp.concatenate([e01, e01, e01], axis=0)
    kern = functools.partial(_ssd_kernel, nblk=nblk, mix=mix)
    xbc_c, z_c, dt_c = lay["xbc"], lay["z"], lay["dt"]
    return pl.pallas_call(
        kern,
        grid=(nb, 2, nblk),
        in_specs=[
            pl.BlockSpec((None, SSD_HALO, cdim),
                         lambda b, d, j: (b, jnp.maximum(conv_blk(b, d, j) * hb - 1, 0), xbc_c)),
            pl.BlockSpec((None, ROW_BLOCK, cdim), lambda b, d, j: (b, conv_blk(b, d, j), xbc_c)),
            pl.BlockSpec((None, SSD_HALO, cdim),
                         lambda b, d, j: (b, jnp.minimum((conv_blk(b, d, j) + 1) * hb, nblk * hb - 1), xbc_c)),
            pl.BlockSpec((None, ROW_BLOCK, mix),
                         lambda b, d, j: (b, jnp.where(d == 0, 0, blk_map(b, d, j)), z_c)),
            pl.BlockSpec((None, ROW_BLOCK, LANES), lambda b, d, j: (b, blk_map(b, d, j), dt_c)),
            pl.BlockSpec((SSD_CONV, cdim), lambda b, d, j: (0, 0)),
            pl.BlockSpec((1, cdim), lambda b, d, j: (0, 0)),
            pl.BlockSpec((None, 1, LANES), lambda b, d, j: (d, 0, 0)),
            pl.BlockSpec((None, 1, LANES), lambda b, d, j: (d, 0, 0)),
            pl.BlockSpec((None, 1, mix), lambda b, d, j: (d, 0, 0)),
            pl.BlockSpec((1, mix), lambda b, d, j: (0, 0)),
            pl.BlockSpec((3 * LANES, mix), lambda b, d, j: (0, 0)),
        ],
        out_specs=pl.BlockSpec((None, ROW_BLOCK, mix),
                               lambda b, d, j: (b, jnp.where(d == 0, 0, blk_map(b, d, j)), 0)),
        out_shape=jax.ShapeDtypeStruct((nb, r, mix), BF16),
        scratch_shapes=[
            pltpu.VMEM((ROW_BLOCK + 2 * SSD_HALO, cdim), F32),
            pltpu.VMEM((r, mix), F32),
            pltpu.VMEM((r, bcw), BF16),
            pltpu.VMEM((r, bcw), BF16),
            pltpu.VMEM((r // SSD_CHUNK, bcw, SSD_CHUNK), F32),
            pltpu.VMEM((r, mix), F32),
            pltpu.VMEM((ROW_BLOCK, mix), F32),
            pltpu.VMEM((SSD_STATE, mix), F32),
        ],
        compiler_params=_params("arbitrary", "arbitrary", "arbitrary"),
        name="ssd",
    )(p3, p3, p3, p3, p3, conv_w, conv_b.reshape(1, cdim), dtb, alog, dsk,
      norm_g.reshape(1, mix), e3)


def _gmlp_kernel(p_ref, g_ref, ws_ref, bias_ref, o_ref, *, mix):
    p = p_ref[...]
    ge = 0.5 * p * (1.0 + jnp.tanh(math.sqrt(2.0 / math.pi) * (p + 0.044715 * (p * p * p))))
    u = ge[:, :mix]
    v = _rms(ge[:, mix:], g_ref[...]).astype(BF16)
    gw = mix // GMLP_GROUPS
    for c in range(ROW_BLOCK // GMLP_CHUNK):
        r0 = c * GMLP_CHUNK
        outs = []
        for g in range(GMLP_GROUPS):
            outs.append(_dot(ws_ref[g], v[r0:r0 + GMLP_CHUNK, g * gw:(g + 1) * gw]))
        mixed = jnp.concatenate(outs, axis=1) + bias_ref[...]
        o_ref[r0:r0 + GMLP_CHUNK, :] = (u[r0:r0 + GMLP_CHUNK, :] * mixed).astype(o_ref.dtype)


def _gmlp(p3, norm_g, ws, bs, lay, *, mix):
    nb, r, _ = p3.shape
    nblk = r // ROW_BLOCK
    bias_full = jnp.repeat(bs.T, mix // GMLP_GROUPS, axis=1)
    gm_c = lay["gm"]
    return pl.pallas_call(
        functools.partial(_gmlp_kernel, mix=mix),
        grid=(nb, nblk),
        in_specs=[
            pl.BlockSpec((None, ROW_BLOCK, 2 * mix), lambda b, j: (b, j, gm_c)),
            pl.BlockSpec((1, mix), lambda b, j: (0, 0)),
            pl.BlockSpec((GMLP_GROUPS, GMLP_CHUNK, GMLP_CHUNK), lambda b, j: (0, 0, 0)),
            pl.BlockSpec((GMLP_CHUNK, mix), lambda b, j: (0, 0)),
        ],
        out_specs=pl.BlockSpec((None, ROW_BLOCK, mix), lambda b, j: (b, j, 0)),
        out_shape=jax.ShapeDtypeStruct((nb, r, mix), BF16),
        compiler_params=_params("arbitrary", "arbitrary"),
        name="gmlp",
    )(p3, norm_g.reshape(1, mix), ws.astype(BF16), bias_full)


def _conf_kernel(prev_ref, cur_ref, next_ref, w_ref, b_ref, lg_ref, lb_ref, o_ref, gw_ref, ph_ref,
                 *, nblk, mix):
    j = pl.program_id(1)

    def glu(v):
        return v[:, :mix] * jax.nn.sigmoid(v[:, mix:])

    has_prev = jnp.logical_and(j != 0, j != 1)
    has_next = jnp.logical_and(j != 0, j != nblk - 1)
    gw_ref[0:CONF_HALO, :] = jnp.where(has_prev, glu(prev_ref[...]), 0.0)
    gw_ref[CONF_HALO:CONF_HALO + ROW_BLOCK, :] = glu(cur_ref[...])
    gw_ref[CONF_HALO + ROW_BLOCK:, :] = jnp.where(has_next, glu(next_ref[...]), 0.0)
    span = ph_ref.shape[1]
    for r in range(1, SUBLANES):
        ph_ref[r - 1] = gw_ref[pl.ds(r, span), :]
    pad = (CONV_KERNEL - 1) // 2
    acc = jnp.broadcast_to(b_ref[...], (ROW_BLOCK, mix))
    for k in range(CONV_KERNEL):
        a, r = divmod(CONF_HALO - pad + k, SUBLANES)
        if r == 0:
            win = gw_ref[pl.ds(a * SUBLANES, ROW_BLOCK), :]
        else:
            win = ph_ref[r - 1, pl.ds(a * SUBLANES, ROW_BLOCK), :]
        acc = acc + w_ref[k:k + 1, :] * win
    mu = jnp.mean(acc, axis=-1, keepdims=True)
    cen = acc - mu
    var = jnp.mean(cen * cen, axis=-1, keepdims=True)
    y = cen * lax.rsqrt(var + EPS) * lg_ref[...] + lb_ref[...]
    o_ref[...] = _silu(y).astype(o_ref.dtype)


def _conf(p3, dw_w, dw_b, ln_g, ln_b, lay, *, mix):
    nb, r, _ = p3.shape
    nblk = r // ROW_BLOCK
    hb = ROW_BLOCK // CONF_HALO
    cv_c = lay["cv"]
    win_rows = ROW_BLOCK + 2 * CONF_HALO
    return pl.pallas_call(
        functools.partial(_conf_kernel, nblk=nblk, mix=mix),
        grid=(nb, nblk),
        in_specs=[
            pl.BlockSpec((None, CONF_HALO, 2 * mix), lambda b, j: (b, jnp.maximum(j * hb - 1, 0), cv_c)),
            pl.BlockSpec((None, ROW_BLOCK, 2 * mix), lambda b, j: (b, j, cv_c)),
            pl.BlockSpec((None, CONF_HALO, 2 * mix),
                         lambda b, j: (b, jnp.minimum((j + 1) * hb, nblk * hb - 1), cv_c)),
            pl.BlockSpec((CONV_KERNEL, mix), lambda b, j: (0, 0)),
            pl.BlockSpec((1, mix), lambda b, j: (0, 0)),
            pl.BlockSpec((1, mix), lambda b, j: (0, 0)),
            pl.BlockSpec((1, mix), lambda b, j: (0, 0)),
        ],
        out_specs=pl.BlockSpec((None, ROW_BLOCK, mix), lambda b, j: (b, j, 0)),
        out_shape=jax.ShapeDtypeStruct((nb, r, mix), BF16),
        scratch_shapes=[pltpu.VMEM((win_rows, mix), F32),
                        pltpu.VMEM((SUBLANES - 1, win_rows - SUBLANES, mix), F32)],
        compiler_params=_params("arbitrary", "arbitrary"),
        name="conf",
    )(p3, p3, p3, dw_w, dw_b.reshape(1, mix), ln_g.reshape(1, mix), ln_b.reshape(1, mix))


def _rope(x, cos, sin_signed):
    lane = lax.broadcasted_iota(jnp.int32, x.shape, 1)
    first = (lane & (HEAD_DIM // 4)) == 0
    partner = jnp.where(first, pltpu.roll(x, HEAD_DIM - HEAD_DIM // 4, 1), pltpu.roll(x, HEAD_DIM // 4, 1))
    return x * cos + partner * sin_signed


def _lane_blocks(x):
    return [x[:, c:c + LANES] for c in range(0, x.shape[1], LANES)]


def _attn_kernel(sink_ref, q_ref, k_ref, v_ref, cos_ref, sin_ref, o_ref, kr_ref, vb_ref, *,
                 ctx_len, n_kv, rows):
    j = pl.program_id(1)
    blk = WINDOW
    scale = HEAD_DIM ** -0.5

    @pl.when(j == 0)
    def _():
        for g in range(n_kv):
            kr_ref[:, g * HEAD_DIM:(g + 1) * HEAD_DIM] = _rope(
                k_ref[:, g * HEAD_DIM:(g + 1) * HEAD_DIM], cos_ref[...], sin_ref[...]).astype(BF16)
        vb_ref[...] = v_ref[...].astype(BF16)

    q0 = pl.multiple_of(j * blk, blk)
    cos_q = cos_ref[pl.ds(q0, blk), :]
    sin_q = sin_ref[pl.ds(q0, blk), :]
    band0 = pl.multiple_of(jnp.clip((j - 1) * blk, 0, rows - 3 * blk), blk)
    qpos = q0 + lax.broadcasted_iota(jnp.int32, (blk, 3 * blk), 0)
    kpos = band0 + lax.broadcasted_iota(jnp.int32, (blk, 3 * blk), 1)
    ok = jnp.logical_and(jnp.logical_and(qpos >= ctx_len, kpos >= ctx_len), jnp.abs(kpos - qpos) <= WINDOW)
    bias = jnp.where(ok, 0.0, -jnp.inf)
    bias = jnp.concatenate([bias] * KV_RATIO, axis=0)
    hrow = lax.broadcasted_iota(jnp.int32, (KV_RATIO * blk, 1), 0) // blk

    for g in range(n_kv):
        gc = slice(g * HEAD_DIM, (g + 1) * HEAD_DIM)
        k_band = kr_ref[pl.ds(band0, 3 * blk), gc]
        v_band = vb_ref[pl.ds(band0, 3 * blk), gc]
        k_ctx = kr_ref[0:ctx_len, gc]
        v_ctx = vb_ref[0:ctx_len, gc]
        qs = []
        sink = jnp.zeros((KV_RATIO * blk, 1), F32)
        for r in range(KV_RATIO):
            h = g * KV_RATIO + r
            qs.append(_rope(q_ref[:, h * HEAD_DIM:(h + 1) * HEAD_DIM], cos_q, sin_q).astype(BF16))
            sink = jnp.where(hrow == r, sink_ref[h], sink)
        qg = jnp.concatenate(qs, axis=0)
        s_band = _dot_nt(qg, k_band) * scale + bias
        s_ctx = _dot_nt(qg, k_ctx) * scale
        mx = functools.reduce(jnp.maximum, _lane_blocks(s_band) + _lane_blocks(s_ctx))
        m = jnp.maximum(jnp.max(mx, axis=-1, keepdims=True), sink)
        p_band = jnp.exp(s_band - m)
        p_ctx = jnp.exp(s_ctx - m)
        psum = functools.reduce(lambda u, w: u + w, _lane_blocks(p_band) + _lane_blocks(p_ctx))
        den = jnp.sum(psum, axis=-1, keepdims=True) + jnp.exp(sink - m)
        o = (_dot(p_band.astype(BF16), v_band) + _dot(p_ctx.astype(BF16), v_ctx)) / den
        for r in range(KV_RATIO):
            h = g * KV_RATIO + r
            o_ref[:, h * HEAD_DIM:(h + 1) * HEAD_DIM] = o[r * blk:(r + 1) * blk, :].astype(o_ref.dtype)


def _attn(p3, sink, cos_t, sin_t, lay, *, mix, ctx_len):
    nb, r, _ = p3.shape
    n_heads = mix // HEAD_DIM
    n_kv = n_heads // KV_RATIO
    kvw = n_kv * HEAD_DIM
    q_c, k_c, v_c = lay["q"], lay["k"], lay["v"]
    kern = functools.partial(_attn_kernel, ctx_len=ctx_len, n_kv=n_kv, rows=r)
    return pl.pallas_call(
        kern,
        grid=(nb, r // WINDOW),
        in_specs=[
            pl.BlockSpec(memory_space=pltpu.SMEM),
            pl.BlockSpec((None, WINDOW, mix), lambda b, j: (b, j, q_c)),
            pl.BlockSpec((None, r, kvw), lambda b, j: (b, 0, k_c)),
            pl.BlockSpec((None, r, kvw), lambda b, j: (b, 0, v_c)),
            pl.BlockSpec((r, HEAD_DIM), lambda b, j: (0, 0)),
            pl.BlockSpec((r, HEAD_DIM), lambda b, j: (0, 0)),
        ],
        out_specs=pl.BlockSpec((None, WINDOW, mix), lambda b, j: (b, j, 0)),
        out_shape=jax.ShapeDtypeStruct((nb, r, mix), BF16),
        scratch_shapes=[pltpu.VMEM((r, kvw), BF16), pltpu.VMEM((r, kvw), BF16)],
        compiler_params=_params("arbitrary", "arbitrary"),
        name="attn",
    )(sink, p3, p3, p3, cos_t, sin_t)


def _rope_tables(seq, ctx_len):
    quarter = HEAD_DIM // 4
    t = jnp.arange(seq, dtype=jnp.int32)
    rows = (t // GRID_W).astype(F32)
    cols = (t % GRID_W).astype(F32)
    freqs = ROPE_BASE ** (-jnp.arange(quarter, dtype=F32) / quarter)
    ang_r = rows[:, None] * freqs
    ang_c = cols[:, None] * freqs
    cos = jnp.concatenate([jnp.cos(ang_r)] * 2 + [jnp.cos(ang_c)] * 2, axis=1)
    sin = jnp.concatenate([-jnp.sin(ang_r), jnp.sin(ang_r), -jnp.sin(ang_c), jnp.sin(ang_c)], axis=1)
    cos = jnp.concatenate([jnp.ones((ctx_len, HEAD_DIM), F32), cos], axis=0)
    sin = jnp.concatenate([jnp.zeros((ctx_len, HEAD_DIM), F32), sin], axis=0)
    return cos, sin


def _merge_kernel(h_ref, ya_ref, yb_ref, yc_ref, yd_ref, wg_ref, bg_ref, wb_ref, o_ref):
    h = h_ref[...]
    acc = None
    for b, y_ref in enumerate((ya_ref, yb_ref, yc_ref, yd_ref)):
        gate = jax.nn.sigmoid(_dot(h, wg_ref[b]) + bg_ref[b])
        term = gate * _dot(y_ref[...], wb_ref[b])
        acc = term if acc is None else acc + term
    o_ref[...] = acc.astype(o_ref.dtype)


def _merge(h2, ys, w_gate, b_gate, w_branch, layer, *, tm):
    n, d = h2.shape
    mix = w_branch.shape[2]
    tn = MERGE_TN
    y_spec = pl.BlockSpec((tm, mix), lambda i, j: (i, 0))
    return pl.pallas_call(
        _merge_kernel,
        grid=(n // tm, d // tn),
        in_specs=[
            pl.BlockSpec((tm, d), lambda i, j: (i, 0)),
            y_spec, y_spec, y_spec, y_spec,
            pl.BlockSpec((None, N_BRANCH, d, tn), lambda i, j: (layer, 0, 0, j)),
            pl.BlockSpec((N_BRANCH, 1, tn), lambda i, j: (0, 0, j)),
            pl.BlockSpec((None, N_BRANCH, mix, tn), lambda i, j: (layer, 0, 0, j)),
        ],
        out_specs=pl.BlockSpec((tm, tn), lambda i, j: (i, j)),
        out_shape=jax.ShapeDtypeStruct((n, d), BF16),
        compiler_params=_params("arbitrary", "arbitrary"),
        name="merge",
    )(h2, *ys, w_gate, b_gate.reshape(N_BRANCH, 1, d), w_branch)


def _outproj_kernel(x_ref, ga_b, ga_c, m_ref, w_ref, o_ref, *, tiles_per_batch, ctx_len):
    i = pl.program_id(0)
    tm, d = x_ref.shape
    m = m_ref[...]
    for c in range(0, d, OUT_TN):
        y = _dot(m, w_ref[:, c:c + OUT_TN])
        for rows, gate in _row_parts(i, tiles_per_batch, ctx_len, tm, ga_c, ga_b):
            o_ref[rows, c:c + OUT_TN] = x_ref[rows, c:c + OUT_TN] + gate[:, c:c + OUT_TN] * y[rows, :]


def _outproj(x2, mods, merged, w_out, layer, *, tm, tiles_per_batch, n_batch, ctx_len):
    n, d = x2.shape
    _, _, gate = mods
    bs, cs = _mod_specs(tiles_per_batch, n_batch, d)
    kern = functools.partial(_outproj_kernel, tiles_per_batch=tiles_per_batch, ctx_len=ctx_len)
    return pl.pallas_call(
        kern,
        grid=(n // tm,),
        in_specs=[
            pl.BlockSpec((tm, d), lambda i: (i, 0)),
            bs, cs,
            pl.BlockSpec((tm, d), lambda i: (i, 0)),
            pl.BlockSpec((None, d, d), lambda i: (layer, 0, 0), pipeline_mode=pl.Buffered(1)),
        ],
        out_specs=pl.BlockSpec((tm, d), lambda i: (i, 0)),
        out_shape=jax.ShapeDtypeStruct((n, d), F32),
        compiler_params=_params("arbitrary"),
        name="outproj",
    )(x2, gate, gate, merged, w_out)


def _final_kernel(x_ref, g_ref, o_ref):
    o_ref[...] = _rms(x_ref[...], g_ref[...])


def _final(x3, g, *, ctx_len, seq):
    nb, _, d = x3.shape
    off = ctx_len // ROW_BLOCK
    return pl.pallas_call(
        _final_kernel,
        grid=(nb, seq // ROW_BLOCK),
        in_specs=[
            pl.BlockSpec((None, ROW_BLOCK, d), lambda b, j: (b, j + off, 0)),
            pl.BlockSpec((1, d), lambda b, j: (0, 0)),
        ],
        out_specs=pl.BlockSpec((None, ROW_BLOCK, d), lambda b, j: (b, j, 0)),
        out_shape=jax.ShapeDtypeStruct((nb, seq, d), F32),
        compiler_params=_params("arbitrary", "arbitrary"),
        name="final",
    )(x3, g.reshape(1, d))


def _proj_layout(mix, n_kv):
    cdim = mix + 2 * SSD_GROUPS * SSD_STATE
    heads = mix // SSD_HEADDIM
    kvw = n_kv * HEAD_DIM
    sizes = dict(z=mix, xbc=cdim, dt=heads, gm=2 * mix, cv=2 * mix, q=mix, k=kvw, v=kvw)
    src, o = {}, 0
    for name in ("z", "xbc", "dt", "gm", "cv", "q", "k", "v"):
        src[name] = (o, sizes[name])
        o += sizes[name]
    order = ("xbc", "gm", "cv", "z", "q", "k", "v", "dt")
    dst, o = {}, 0
    for name in order:
        width = LANES if name == "dt" else sizes[name]
        assert o % width == 0
        dst[name] = o // width
        o += width
    total = -(-o // INPROJ_TN) * INPROJ_TN
    return src, order, dst, total


def _permute_w_in(w_in, src, order, total):
    cols = []
    for name in order:
        s, w = src[name]
        cols.append(w_in[:, :, s:s + w].astype(BF16))
    out = jnp.concatenate(cols, axis=2)
    return jnp.pad(out, ((0, 0), (0, 0), (0, total - out.shape[2])))


def kernel(x, c, ctx, c_ctx, w_ada, b_ada, ffn1_norm, ffn1_w13, ffn1_w2, mix_norm, w_in, w_gate, b_gate, w_branch, w_out, ssd_conv_w, ssd_conv_b, ssd_dt_bias, ssd_a_log, ssd_d, ssd_norm, gmlp_norm, gmlp_ws, gmlp_bs, conv_dw_w, conv_dw_b, conv_ln_g, conv_ln_b, attn_sink, ffn2_norm, ffn2_w13, ffn2_w2, final_norm):
    nb, seq, d = x.shape
    ctx_len = ctx.shape[1]
    depth = w_ada.shape[0]
    mix = w_branch.shape[2]
    n_kv = mix // HEAD_DIM // KV_RATIO
    r = ctx_len + seq
    assert ctx_len == ROW_BLOCK and seq % ROW_BLOCK == 0 and nb + 1 <= MOD_ROWS
    tm = TOKEN_TILE
    assert r % tm == 0
    tiles_per_batch = r // tm
    tile_kw = dict(tm=tm, tiles_per_batch=tiles_per_batch, n_batch=nb, ctx_len=ctx_len)

    src, order, dst, total = _proj_layout(mix, n_kv)
    cos_t, sin_t = _rope_tables(seq, ctx_len)

    ffn1_w13_b, ffn1_w2_b = ffn1_w13.astype(BF16), ffn1_w2.astype(BF16)
    ffn2_w13_b, ffn2_w2_b = ffn2_w13.astype(BF16), ffn2_w2.astype(BF16)
    w_in_b = _permute_w_in(w_in, src, order, total)
    w_gate_b, w_branch_b, w_out_b = w_gate.astype(BF16), w_branch.astype(BF16), w_out.astype(BF16)

    c_all = jnp.concatenate([c, c_ctx[None, :], jnp.zeros((MOD_ROWS - nb - 1, d), F32)], axis=0)
    mod_all = _ada(c_all, w_ada, b_ada).reshape(depth, MOD_ROWS, N_MOD, 1, d)

    xcur = jnp.concatenate([ctx, x], axis=1).reshape(nb * r, d)
    for l in range(depth):
        def mods(k, l=l):
            return tuple(mod_all[l, :, 3 * k + t] for t in range(3))

        xcur = _ffn(xcur, mods(0), ffn1_norm[l], ffn1_w13_b, ffn1_w2_b, l, **tile_kw)
        p2, h2 = _inproj(xcur, mods(1), mix_norm[l], w_in_b, l, **tile_kw)
        p3 = p2.reshape(nb, r, total)
        ya = _ssd(p3, ssd_conv_w[l], ssd_conv_b[l], ssd_dt_bias[l], ssd_a_log[l], ssd_d[l],
                  ssd_norm[l], dst, mix=mix)
        yb = _gmlp(p3, gmlp_norm[l], gmlp_ws[l], gmlp_bs[l], dst, mix=mix)
        yc = _conf(p3, conv_dw_w[l], conv_dw_b[l], conv_ln_g[l], conv_ln_b[l], dst, mix=mix)
        yd = _attn(p3, attn_sink[l], cos_t, sin_t, dst, mix=mix, ctx_len=ctx_len)
        ys = tuple(y.reshape(nb * r, mix) for y in (ya, yb, yc, yd))
        merged = _merge(h2, ys, w_gate_b, b_gate[l], w_branch_b, l, tm=tm)
        xcur = _outproj(xcur, mods(1), merged, w_out_b, l, **tile_kw)
        xcur = _ffn(xcur, mods(2), ffn2_norm[l], ffn2_w13_b, ffn2_w2_b, l, **tile_kw)
    return _final(xcur.reshape(nb, r, d), final_norm, ctx_len=ctx_len, seq=seq)
```

```python
import functools
import math

import jax
import jax.numpy as jnp
from jax import lax
from jax.experimental import pallas as pl
from jax.experimental.pallas import tpu as pltpu

F32 = jnp.float32
BF16 = jnp.bfloat16

GRID_W = 64
N_BRANCH = 4
N_MOD = 9
HALF_STEP = 0.5
SSD_HEADDIM = 64
SSD_GROUPS = 4
SSD_STATE = 128
SSD_CONV = 5
SSD_CHUNK = 128
GMLP_GROUPS = 8
GMLP_CHUNK = 128
CONV_KERNEL = 31
HEAD_DIM = 128
KV_RATIO = 4
WINDOW = 128
ROPE_BASE = 10000.0
EPS = 1e-6

LANES = 128
SUBLANES = 8
VMEM_LIMIT_BYTES = 56 * 1024 * 1024

TOKEN_TILE = 768
FFN_TF = 512
INPROJ_TN = 1536
MERGE_TN = 512
OUT_TN = 512
ADA_TN = 1024
ROW_BLOCK = 256
SSD_HALO = 8
CONF_HALO = 16
MOD_ROWS = 24


def _params(*sem):
    return pltpu.CompilerParams(dimension_semantics=sem, vmem_limit_bytes=VMEM_LIMIT_BYTES)


def _silu(x):
    return x * jax.nn.sigmoid(x)


def _dot(a, b):
    return jnp.dot(a, b, preferred_element_type=F32)


def _dot_nt(a, b):
    return lax.dot_general(a, b, (((1,), (1,)), ((), ())), preferred_element_type=F32)


def _rms(x, g):
    return x * lax.rsqrt(jnp.mean(x * x, axis=-1, keepdims=True) + EPS) * g


def _split3(v):
    hi = v.astype(BF16)
    r1 = v - hi.astype(F32)
    mid = r1.astype(BF16)
    lo = (r1 - mid.astype(F32)).astype(BF16)
    return hi, mid, lo


def _exact_dot_left(m01, v):
    hi, mid, lo = _split3(v)
    return _dot(m01, hi) + _dot(m01, mid) + _dot(m01, lo)


def _row_parts(tile, tiles_per_batch, ctx_len, tm, ctx_ref, batch_ref):
    first = tile % tiles_per_batch == 0
    top = jnp.where(first, ctx_ref[...], batch_ref[...])
    return ((slice(0, ctx_len), top), (slice(ctx_len, tm), batch_ref[...]))


def _modnorm_groups(h_ref, x_ref, g_ref, tile, tiles_per_batch, ctx_len, sh_b, sh_c, sc_b, sc_c):
    tm = x_ref.shape[0]
    first = tile % tiles_per_batch == 0
    for r0 in range(0, tm, ctx_len):
        rows = slice(r0, r0 + ctx_len)
        if r0 == 0:
            shift = jnp.where(first, sh_c[...], sh_b[...])
            scale = jnp.where(first, sc_c[...], sc_b[...])
        else:
            shift, scale = sh_b[...], sc_b[...]
        h_ref[rows, :] = (_rms(x_ref[rows, :], g_ref[...]) * (1.0 + scale) + shift).astype(h_ref.dtype)
        yield rows


def _ada_kernel(c_ref, w_ref, b_ref, o_ref):
    a = _silu(c_ref[...]).astype(BF16)
    o_ref[...] = _dot(a, w_ref[...].astype(BF16)) + b_ref[...]


def _ada(c_all, w_ada, b_ada):
    depth, d, nm = w_ada.shape
    tn = ADA_TN
    return pl.pallas_call(
        _ada_kernel,
        grid=(depth, nm // tn),
        in_specs=[
            pl.BlockSpec((MOD_ROWS, d), lambda l, n: (0, 0)),
            pl.BlockSpec((None, d, tn), lambda l, n: (l, 0, n)),
            pl.BlockSpec((None, 1, tn), lambda l, n: (l, 0, n)),
        ],
        out_specs=pl.BlockSpec((None, MOD_ROWS, tn), lambda l, n: (l, 0, n)),
        out_shape=jax.ShapeDtypeStruct((depth, MOD_ROWS, nm), F32),
        compiler_params=_params("arbitrary", "arbitrary"),
        name="ada",
    )(c_all, w_ada, b_ada.reshape(depth, 1, nm))


def _mod_specs(tiles_per_batch, n_batch, d):
    def batch_map(i, *_):
        return (i // tiles_per_batch, 0, 0)

    def ctx_map(i, *_):
        return (n_batch, 0, 0)

    return (pl.BlockSpec((None, 1, d), batch_map), pl.BlockSpec((None, 1, d), ctx_map))


def _ffn_kernel(x_ref, sh_b, sh_c, sc_b, sc_c, ga_b, ga_c, g_ref, w1_ref, w3_ref, w2_ref,
                o_ref, h_ref, *, tiles_per_batch, ctx_len, nf):
    i = pl.program_id(0)
    f = pl.program_id(1)
    tm = x_ref.shape[0]

    def swiglu(h):
        a = _dot(h, w1_ref[...])
        b = _dot(h, w3_ref[...])
        return _dot((_silu(a) * b).astype(BF16), w2_ref[...])

    @pl.when(f == 0)
    def _():
        for rows in _modnorm_groups(h_ref, x_ref, g_ref, i, tiles_per_batch, ctx_len,
                                    sh_b, sh_c, sc_b, sc_c):
            o_ref[rows, :] = swiglu(h_ref[rows, :])

    @pl.when(f > 0)
    def _():
        o_ref[...] += swiglu(h_ref[...])

    @pl.when(f == nf - 1)
    def _():
        for rows, gate in _row_parts(i, tiles_per_batch, ctx_len, tm, ga_c, ga_b):
            o_ref[rows, :] = x_ref[rows, :] + HALF_STEP * gate * o_ref[rows, :]


def _ffn(x2, mods, g, w13, w2, layer, *, tm, tiles_per_batch, n_batch, ctx_len):
    n, d = x2.shape
    fdim = w2.shape[1]
    tf = FFN_TF
    nf = fdim // tf
    shift, scale, gate = mods
    bs, cs = _mod_specs(tiles_per_batch, n_batch, d)
    kern = functools.partial(_ffn_kernel, tiles_per_batch=tiles_per_batch, ctx_len=ctx_len, nf=nf)
    return pl.pallas_call(
        kern,
        grid=(n // tm, nf),
        in_specs=[
            pl.BlockSpec((tm, d), lambda i, f: (i, 0)),
            bs, cs, bs, cs, bs, cs,
            pl.BlockSpec((1, d), lambda i, f: (0, 0)),
            pl.BlockSpec((None, d, tf), lambda i, f: (layer, 0, f)),
            pl.BlockSpec((None, d, tf), lambda i, f: (layer, 0, f + nf)),
            pl.BlockSpec((None, tf, d), lambda i, f: (layer, f, 0)),
        ],
        out_specs=pl.BlockSpec((tm, d), lambda i, f: (i, 0)),
        out_shape=jax.ShapeDtypeStruct((n, d), F32),
        scratch_shapes=[pltpu.VMEM((tm, d), BF16)],
        compiler_params=_params("arbitrary", "arbitrary"),
        name="ffn",
    )(x2, shift, shift, scale, scale, gate, gate, g.reshape(1, d), w13, w13, w2)


def _inproj_kernel(x_ref, sh_b, sh_c, sc_b, sc_c, g_ref, w_ref, o_ref, h_ref, *,
                   tiles_per_batch, ctx_len):
    i = pl.program_id(0)
    n = pl.program_id(1)

    @pl.when(n == 0)
    def _():
        for rows in _modnorm_groups(h_ref, x_ref, g_ref, i, tiles_per_batch, ctx_len,
                                    sh_b, sh_c, sc_b, sc_c):
            o_ref[rows, :] = _dot(h_ref[rows, :], w_ref[...])

    @pl.when(n > 0)
    def _():
        o_ref[...] = _dot(h_ref[...], w_ref[...])


def _inproj(x2, mods, g, w, layer, *, tm, tiles_per_batch, n_batch, ctx_len):
    n, d = x2.shape
    width = w.shape[2]
    tn = INPROJ_TN
    shift, scale, _ = mods
    bs, cs = _mod_specs(tiles_per_batch, n_batch, d)
    kern = functools.partial(_inproj_kernel, tiles_per_batch=tiles_per_batch, ctx_len=ctx_len)
    return pl.pallas_call(
        kern,
        grid=(n // tm, width // tn),
        in_specs=[
            pl.BlockSpec((tm, d), lambda i, j: (i, 0)),
            bs, cs, bs, cs,
            pl.BlockSpec((1, d), lambda i, j: (0, 0)),
            pl.BlockSpec((None, d, tn), lambda i, j: (layer, 0, j)),
        ],
        out_specs=[pl.BlockSpec((tm, tn), lambda i, j: (i, j)),
                   pl.BlockSpec((tm, d), lambda i, j: (i, 0))],
        out_shape=[jax.ShapeDtypeStruct((n, width), F32), jax.ShapeDtypeStruct((n, d), BF16)],
        compiler_params=_params("arbitrary", "arbitrary"),
        name="inproj",
    )(x2, shift, shift, scale, scale, g.reshape(1, d), w)


def _ssd_block_of(d, j, nblk):
    return jnp.where(d == 0, j, jnp.where(j == 0, 0, nblk - j))


def _ssd_kernel(prev_ref, cur_ref, next_ref, z_ref, dt_ref, cw_ref, cb_ref, dtb_ref, alog_ref,
                dsk_ref, ng_ref, e3_ref, o_ref, xw_ref, x_ref, b_ref, c_ref, bt_ref, yf_ref, st_ref,
                *, nblk, mix):
    d = pl.program_id(1)
    j = pl.program_id(2)
    blk = _ssd_block_of(d, j, nblk)
    q = SSD_CHUNK
    gw = SSD_STATE
    bcw = SSD_GROUPS * gw
    hpg = mix // SSD_HEADDIM // SSD_GROUPS
    nch = ROW_BLOCK // q
    row0 = pl.multiple_of(blk * ROW_BLOCK, ROW_BLOCK)
    pad = (SSD_CONV - 1) // 2

    @pl.when(j == 0)
    def _():
        st_ref[...] = jnp.zeros_like(st_ref)

    ii = lax.broadcasted_iota(jnp.int32, (q, q), 0)
    jj = lax.broadcasted_iota(jnp.int32, (q, q), 1)
    lane = lax.broadcasted_iota(jnp.int32, (q, LANES), 1)
    lo_half = lane < SSD_HEADDIM

    def conv_chunk(ci):
        base = ci * q
        acc = jnp.broadcast_to(cb_ref[...], (q, cb_ref.shape[1]))
        for k in range(SSD_CONV):
            acc = acc + cw_ref[k:k + 1, :] * xw_ref[pl.ds(SSD_HALO - pad + k + base, q), :]
        hs = _silu(acc)
        grow = pl.multiple_of(row0 + base, q)
        x_ref[pl.ds(grow, q), :] = hs[:, 0:mix]
        b_ref[pl.ds(grow, q), :] = hs[:, mix:mix + bcw].astype(BF16)
        c_ref[pl.ds(grow, q), :] = hs[:, mix + bcw:mix + 2 * bcw].astype(BF16)
        for g in range(SSD_GROUPS):
            bt_ref[blk * nch + ci, g * gw:(g + 1) * gw, :] = hs[:, mix + g * gw:mix + (g + 1) * gw].T

    def scan_chunk(chunk, backward):
        tri = (jj >= ii) if backward else (jj <= ii)
        tri_b = jnp.where(tri, 1.0, 0.0).astype(BF16)
        a_neg = -jnp.exp(alog_ref[...])
        e3 = e3_ref[...]
        r0 = chunk * q
        grow = pl.multiple_of(row0 + r0, q)
        x = x_ref[pl.ds(grow, q), :]
        dt_raw = dt_ref[r0:r0 + q, :] + dtb_ref[...]
        dt = jnp.maximum(dt_raw, 0.0) + jnp.log1p(jnp.exp(-jnp.abs(dt_raw)))
        a = dt * a_neg
        cum = _exact_dot_left(tri_b, a)
        total = jnp.sum(a, axis=0, keepdims=True)
        cum_t = cum.T
        ecum = jnp.exp(cum)
        dt_t = dt.T
        s_t = (dt * jnp.exp(total - cum)).T
        v = jnp.concatenate([ecum, jnp.broadcast_to(jnp.exp(total), (SUBLANES, LANES))], axis=0)
        ex = _dot(jnp.concatenate(_split3(v), axis=1), e3)
        ec_x, cd_x = ex[0:q], ex[q:q + 1]
        ys = []
        for g in range(SSD_GROUPS):
            bg_b = b_ref[pl.ds(grow, q), g * gw:(g + 1) * gw]
            cg_b = c_ref[pl.ds(grow, q), g * gw:(g + 1) * gw]
            btg = bt_ref[blk * nch + chunk, g * gw:(g + 1) * gw, :]
            cb = _dot_nt(cg_b, bg_b)
            c0 = g * hpg * SSD_HEADDIM
            cw = hpg * SSD_HEADDIM
            y_off = _dot(cg_b, st_ref[:, c0:c0 + cw].astype(BF16)) * ec_x[:, c0:c0 + cw]
            y_pairs = []
            for pr in range(hpg // 2):
                l0 = c0 + pr * LANES
                xp = x[:, l0:l0 + LANES]
                ms, bts, xms = [], [], []
                for half in range(2):
                    h = g * hpg + pr * 2 + half
                    seg = cum[:, h:h + 1] - cum_t[h:h + 1, :]
                    dec = jnp.exp(jnp.where(tri, seg, -jnp.inf))
                    keep = lo_half if half == 0 else jnp.logical_not(lo_half)
                    ms.append((cb * dec * dt_t[h:h + 1, :]).astype(BF16))
                    bts.append((btg * s_t[h:h + 1, :]).astype(BF16))
                    xms.append(jnp.where(keep, xp, 0.0).astype(BF16))
                xm = jnp.concatenate(xms, axis=0)
                y_pairs.append(_dot(jnp.concatenate(ms, axis=1), xm))
                st_new = _dot(jnp.concatenate(bts, axis=1), xm)
                st_ref[:, l0:l0 + LANES] = st_ref[:, l0:l0 + LANES] * cd_x[:, l0:l0 + LANES] + st_new
            ys.append(jnp.concatenate(y_pairs, axis=1) + y_off)
        y = jnp.concatenate(ys, axis=1) + x * dsk_ref[...]
        if backward:
            yt = (yf_ref[pl.ds(grow, q), :] + y) * _silu(z_ref[r0:r0 + q, :])
            o_ref[r0:r0 + q, :] = _rms(yt, ng_ref[...]).astype(o_ref.dtype)
        else:
            yf_ref[pl.ds(grow, q), :] = y

    @pl.when(d == 0)
    def _():
        has_prev = jnp.logical_and(blk != 0, blk != 1)
        has_next = jnp.logical_and(blk != 0, blk != nblk - 1)
        xw_ref[0:SSD_HALO, :] = jnp.where(has_prev, prev_ref[...], 0.0)
        xw_ref[SSD_HALO:SSD_HALO + ROW_BLOCK, :] = cur_ref[...]
        xw_ref[SSD_HALO + ROW_BLOCK:, :] = jnp.where(has_next, next_ref[...], 0.0)
        for ci in range(nch):
            conv_chunk(ci)
            scan_chunk(ci, False)

    @pl.when(d == 1)
    def _():
        for ci in range(nch):
            scan_chunk(nch - 1 - ci, True)


def _ssd(p3, conv_w, conv_b, dt_bias, a_log, d_skip, norm_g, lay, *, mix):
    nb, r, _ = p3.shape
    nblk = r // ROW_BLOCK
    cdim = conv_w.shape[1]
    heads = mix // SSD_HEADDIM
    hb = ROW_BLOCK // SSD_HALO
    bcw = SSD_GROUPS * SSD_STATE

    def blk_map(b, d, j):
        return _ssd_block_of(d, j, nblk)

    def conv_blk(b, d, j):
        return jnp.where(d == 0, j, nblk - 1)

    pad_h = LANES - heads
    dtb = jnp.pad(dt_bias, ((0, 0), (0, pad_h))).reshape(2, 1, LANES)
    alog = jnp.pad(a_log, ((0, 0), (0, pad_h))).reshape(2, 1, LANES)
    dsk = jnp.repeat(d_skip, SSD_HEADDIM, axis=1).reshape(2, 1, mix)
    e01 = (jnp.arange(LANES)[:, None] == (jnp.arange(mix)[None, :] // SSD_HEADDIM)).astype(BF16)
    e3 = jnp.concatenate([e01, e01, e01], axis=0)
    kern = functools.partial(_ssd_kernel, nblk=nblk, mix=mix)
    xbc_c, z_c, dt_c = lay["xbc"], lay["z"], lay["dt"]
    return pl.pallas_call(
        kern,
        grid=(nb, 2, nblk),
        in_specs=[
            pl.BlockSpec((None, SSD_HALO, cdim),
                         lambda b, d, j: (b, jnp.maximum(conv_blk(b, d, j) * hb - 1, 0), xbc_c)),
            pl.BlockSpec((None, ROW_BLOCK, cdim), lambda b, d, j: (b, conv_blk(b, d, j), xbc_c)),
            pl.BlockSpec((None, SSD_HALO, cdim),
                         lambda b, d, j: (b, jnp.minimum((conv_blk(b, d, j) + 1) * hb, nblk * hb - 1), xbc_c)),
            pl.BlockSpec((None, ROW_BLOCK, mix),
                         lambda b, d, j: (b, jnp.where(d == 0, 0, blk_map(b, d, j)), z_c)),
            pl.BlockSpec((None, ROW_BLOCK, LANES), lambda b, d, j: (b, blk_map(b, d, j), dt_c)),
            pl.BlockSpec((SSD_CONV, cdim), lambda b, d, j: (0, 0)),
            pl.BlockSpec((1, cdim), lambda b, d, j: (0, 0)),
            pl.BlockSpec((None, 1, LANES), lambda b, d, j: (d, 0, 0)),
            pl.BlockSpec((None, 1, LANES), lambda b, d, j: (d, 0, 0)),
            pl.BlockSpec((None, 1, mix), lambda b, d, j: (d, 0, 0)),
            pl.BlockSpec((1, mix), lambda b, d, j: (0, 0)),
            pl.BlockSpec((3 * LANES, mix), lambda b, d, j: (0, 0)),
        ],
        out_specs=pl.BlockSpec((None, ROW_BLOCK, mix),
                               lambda b, d, j: (b, jnp.where(d == 0, 0, blk_map(b, d, j)), 0)),
        out_shape=jax.ShapeDtypeStruct((nb, r, mix), BF16),
        scratch_shapes=[
            pltpu.VMEM((ROW_BLOCK + 2 * SSD_HALO, cdim), F32),
            pltpu.VMEM((r, mix), F32),
            pltpu.VMEM((r, bcw), BF16),
            pltpu.VMEM((r, bcw), BF16),
            pltpu.VMEM((r // SSD_CHUNK, bcw, SSD_CHUNK), F32),
            pltpu.VMEM((r, mix), F32),
            pltpu.VMEM((SSD_STATE, mix), F32),
        ],
        compiler_params=_params("arbitrary", "arbitrary", "arbitrary"),
        name="ssd",
    )(p3, p3, p3, p3, p3, conv_w, conv_b.reshape(1, cdim), dtb, alog, dsk,
      norm_g.reshape(1, mix), e3)


def _gmlp_kernel(p_ref, g_ref, ws_ref, bias_ref, o_ref, *, mix):
    p = p_ref[...]
    ge = 0.5 * p * (1.0 + jnp.tanh(math.sqrt(2.0 / math.pi) * (p + 0.044715 * (p * p * p))))
    u = ge[:, :mix]
    v = _rms(ge[:, mix:], g_ref[...]).astype(BF16)
    gw = mix // GMLP_GROUPS
    for c in range(ROW_BLOCK // GMLP_CHUNK):
        r0 = c * GMLP_CHUNK
        outs = []
        for g in range(GMLP_GROUPS):
            outs.append(_dot(ws_ref[g], v[r0:r0 + GMLP_CHUNK, g * gw:(g + 1) * gw]))
        mixed = jnp.concatenate(outs, axis=1) + bias_ref[...]
        o_ref[r0:r0 + GMLP_CHUNK, :] = (u[r0:r0 + GMLP_CHUNK, :] * mixed).astype(o_ref.dtype)


def _gmlp(p3, norm_g, ws, bs, lay, *, mix):
    nb, r, _ = p3.shape
    nblk = r // ROW_BLOCK
    bias_full = jnp.repeat(bs.T, mix // GMLP_GROUPS, axis=1)
    gm_c = lay["gm"]
    return pl.pallas_call(
        functools.partial(_gmlp_kernel, mix=mix),
        grid=(nb, nblk),
        in_specs=[
            pl.BlockSpec((None, ROW_BLOCK, 2 * mix), lambda b, j: (b, j, gm_c)),
            pl.BlockSpec((1, mix), lambda b, j: (0, 0)),
            pl.BlockSpec((GMLP_GROUPS, GMLP_CHUNK, GMLP_CHUNK), lambda b, j: (0, 0, 0)),
            pl.BlockSpec((GMLP_CHUNK, mix), lambda b, j: (0, 0)),
        ],
        out_specs=pl.BlockSpec((None, ROW_BLOCK, mix), lambda b, j: (b, j, 0)),
        out_shape=jax.ShapeDtypeStruct((nb, r, mix), BF16),
        compiler_params=_params("arbitrary", "arbitrary"),
        name="gmlp",
    )(p3, norm_g.reshape(1, mix), ws.astype(BF16), bias_full)


def _conf_kernel(prev_ref, cur_ref, next_ref, w_ref, b_ref, lg_ref, lb_ref, o_ref, gw_ref, ph_ref,
                 *, nblk, mix):
    j = pl.program_id(1)

    def glu(v):
        return v[:, :mix] * jax.nn.sigmoid(v[:, mix:])

    has_prev = jnp.logical_and(j != 0, j != 1)
    has_next = jnp.logical_and(j != 0, j != nblk - 1)
    gw_ref[0:CONF_HALO, :] = jnp.where(has_prev, glu(prev_ref[...]), 0.0)
    gw_ref[CONF_HALO:CONF_HALO + ROW_BLOCK, :] = glu(cur_ref[...])
    gw_ref[CONF_HALO + ROW_BLOCK:, :] = jnp.where(has_next, glu(next_ref[...]), 0.0)
    span = ph_ref.shape[1]
    for r in range(1, SUBLANES):
        ph_ref[r - 1] = gw_ref[pl.ds(r, span), :]
    pad = (CONV_KERNEL - 1) // 2
    acc = jnp.broadcast_to(b_ref[...], (ROW_BLOCK, mix))
    for k in range(CONV_KERNEL):
        a, r = divmod(CONF_HALO - pad + k, SUBLANES)
        if r == 0:
            win = gw_ref[pl.ds(a * SUBLANES, ROW_BLOCK), :]
        else:
            win = ph_ref[r - 1, pl.ds(a * SUBLANES, ROW_BLOCK), :]
        acc = acc + w_ref[k:k + 1, :] * win
    mu = jnp.mean(acc, axis=-1, keepdims=True)
    cen = acc - mu
    var = jnp.mean(cen * cen, axis=-1, keepdims=True)
    y = cen * lax.rsqrt(var + EPS) * lg_ref[...] + lb_ref[...]
    o_ref[...] = _silu(y).astype(o_ref.dtype)


def _conf(p3, dw_w, dw_b, ln_g, ln_b, lay, *, mix):
    nb, r, _ = p3.shape
    nblk = r // ROW_BLOCK
    hb = ROW_BLOCK // CONF_HALO
    cv_c = lay["cv"]
    win_rows = ROW_BLOCK + 2 * CONF_HALO
    return pl.pallas_call(
        functools.partial(_conf_kernel, nblk=nblk, mix=mix),
        grid=(nb, nblk),
        in_specs=[
            pl.BlockSpec((None, CONF_HALO, 2 * mix), lambda b, j: (b, jnp.maximum(j * hb - 1, 0), cv_c)),
            pl.BlockSpec((None, ROW_BLOCK, 2 * mix), lambda b, j: (b, j, cv_c)),
            pl.BlockSpec((None, CONF_HALO, 2 * mix),
                         lambda b, j: (b, jnp.minimum((j + 1) * hb, nblk * hb - 1), cv_c)),
            pl.BlockSpec((CONV_KERNEL, mix), lambda b, j: (0, 0)),
            pl.BlockSpec((1, mix), lambda b, j: (0, 0)),
            pl.BlockSpec((1, mix), lambda b, j: (0, 0)),
            pl.BlockSpec((1, mix), lambda b, j: (0, 0)),
        ],
        out_specs=pl.BlockSpec((None, ROW_BLOCK, mix), lambda b, j: (b, j, 0)),
        out_shape=jax.ShapeDtypeStruct((nb, r, mix), BF16),
        scratch_shapes=[pltpu.VMEM((win_rows, mix), F32),
                        pltpu.VMEM((SUBLANES - 1, win_rows - SUBLANES, mix), F32)],
        compiler_params=_params("arbitrary", "arbitrary"),
        name="conf",
    )(p3, p3, p3, dw_w, dw_b.reshape(1, mix), ln_g.reshape(1, mix), ln_b.reshape(1, mix))


def _rope(x, cos, sin_signed):
    lane = lax.broadcasted_iota(jnp.int32, x.shape, 1)
    first = (lane & (HEAD_DIM // 4)) == 0
    partner = jnp.where(first, pltpu.roll(x, HEAD_DIM - HEAD_DIM // 4, 1), pltpu.roll(x, HEAD_DIM // 4, 1))
    return x * cos + partner * sin_signed


def _lane_blocks(x):
    return [x[:, c:c + LANES] for c in range(0, x.shape[1], LANES)]


def _attn_kernel(sink_ref, q_ref, k_ref, v_ref, cos_ref, sin_ref, o_ref, kr_ref, vb_ref, *,
                 ctx_len, n_kv, rows):
    j = pl.program_id(1)
    blk = WINDOW
    scale = HEAD_DIM ** -0.5

    @pl.when(j == 0)
    def _():
        for g in range(n_kv):
            kr_ref[:, g * HEAD_DIM:(g + 1) * HEAD_DIM] = _rope(
                k_ref[:, g * HEAD_DIM:(g + 1) * HEAD_DIM], cos_ref[...], sin_ref[...]).astype(BF16)
        vb_ref[...] = v_ref[...].astype(BF16)

    hrow = lax.broadcasted_iota(jnp.int32, (KV_RATIO * blk, 1), 0) // blk
    for sb in range(q_ref.shape[0] // blk):
        qrows = slice(sb * blk, (sb + 1) * blk)
        jb = j * (q_ref.shape[0] // blk) + sb
        q0 = pl.multiple_of(jb * blk, blk)
        cos_q = cos_ref[pl.ds(q0, blk), :]
        sin_q = sin_ref[pl.ds(q0, blk), :]
        band0 = pl.multiple_of(jnp.clip((jb - 1) * blk, 0, rows - 3 * blk), blk)
        qpos = q0 + lax.broadcasted_iota(jnp.int32, (blk, 3 * blk), 0)
        kpos = band0 + lax.broadcasted_iota(jnp.int32, (blk, 3 * blk), 1)
        ok = jnp.logical_and(jnp.logical_and(qpos >= ctx_len, kpos >= ctx_len),
                             jnp.abs(kpos - qpos) <= WINDOW)
        bias = jnp.where(ok, 0.0, -jnp.inf)
        bias = jnp.concatenate([bias] * KV_RATIO, axis=0)

        for g in range(n_kv):
            gc = slice(g * HEAD_DIM, (g + 1) * HEAD_DIM)
            k_band = kr_ref[pl.ds(band0, 3 * blk), gc]
            v_band = vb_ref[pl.ds(band0, 3 * blk), gc]
            k_ctx = kr_ref[0:ctx_len, gc]
            v_ctx = vb_ref[0:ctx_len, gc]
            qs = []
            sink = jnp.zeros((KV_RATIO * blk, 1), F32)
            for r in range(KV_RATIO):
                h = g * KV_RATIO + r
                qs.append(_rope(q_ref[qrows, h * HEAD_DIM:(h + 1) * HEAD_DIM], cos_q, sin_q).astype(BF16))
                sink = jnp.where(hrow == r, sink_ref[h], sink)
            qg = jnp.concatenate(qs, axis=0)
            s_band = _dot_nt(qg, k_band) * scale + bias
            s_ctx = _dot_nt(qg, k_ctx) * scale
            mx = functools.reduce(jnp.maximum, _lane_blocks(s_band) + _lane_blocks(s_ctx))
            m = jnp.maximum(jnp.max(mx, axis=-1, keepdims=True), sink)
            p_band = jnp.exp(s_band - m)
            p_ctx = jnp.exp(s_ctx - m)
            psum = functools.reduce(lambda u, w: u + w, _lane_blocks(p_band) + _lane_blocks(p_ctx))
            den = jnp.sum(psum, axis=-1, keepdims=True) + jnp.exp(sink - m)
            o = (_dot(p_band.astype(BF16), v_band) + _dot(p_ctx.astype(BF16), v_ctx)) / den
            for r in range(KV_RATIO):
                h = g * KV_RATIO + r
                o_ref[qrows, h * HEAD_DIM:(h + 1) * HEAD_DIM] = o[r * blk:(r + 1) * blk, :].astype(o_ref.dtype)


def _attn(p3, sink, cos_t, sin_t, lay, *, mix, ctx_len):
    nb, r, _ = p3.shape
    n_heads = mix // HEAD_DIM
    n_kv = n_heads // KV_RATIO
    kvw = n_kv * HEAD_DIM
    q_c, k_c, v_c = lay["q"], lay["k"], lay["v"]
    kern = functools.partial(_attn_kernel, ctx_len=ctx_len, n_kv=n_kv, rows=r)
    return pl.pallas_call(
        kern,
        grid=(nb, r // ROW_BLOCK),
        in_specs=[
            pl.BlockSpec(memory_space=pltpu.SMEM),
            pl.BlockSpec((None, ROW_BLOCK, mix), lambda b, j: (b, j, q_c)),
            pl.BlockSpec((None, r, kvw), lambda b, j: (b, 0, k_c)),
            pl.BlockSpec((None, r, kvw), lambda b, j: (b, 0, v_c)),
            pl.BlockSpec((r, HEAD_DIM), lambda b, j: (0, 0)),
            pl.BlockSpec((r, HEAD_DIM), lambda b, j: (0, 0)),
        ],
        out_specs=pl.BlockSpec((None, ROW_BLOCK, mix), lambda b, j: (b, j, 0)),
        out_shape=jax.ShapeDtypeStruct((nb, r, mix), BF16),
        scratch_shapes=[pltpu.VMEM((r, kvw), BF16), pltpu.VMEM((r, kvw), BF16)],
        compiler_params=_params("arbitrary", "arbitrary"),
        name="attn",
    )(sink, p3, p3, p3, cos_t, sin_t)


def _rope_tables(seq, ctx_len):
    quarter = HEAD_DIM // 4
    t = jnp.arange(seq, dtype=jnp.int32)
    rows = (t // GRID_W).astype(F32)
    cols = (t % GRID_W).astype(F32)
    freqs = ROPE_BASE ** (-jnp.arange(quarter, dtype=F32) / quarter)
    ang_r = rows[:, None] * freqs
    ang_c = cols[:, None] * freqs
    cos = jnp.concatenate([jnp.cos(ang_r)] * 2 + [jnp.cos(ang_c)] * 2, axis=1)
    sin = jnp.concatenate([-jnp.sin(ang_r), jnp.sin(ang_r), -jnp.sin(ang_c), jnp.sin(ang_c)], axis=1)
    cos = jnp.concatenate([jnp.ones((ctx_len, HEAD_DIM), F32), cos], axis=0)
    sin = jnp.concatenate([jnp.zeros((ctx_len, HEAD_DIM), F32), sin], axis=0)
    return cos, sin


def _merge_kernel(h_ref, ya_ref, yb_ref, yc_ref, yd_ref, wg_ref, bg_ref, wb_ref, o_ref):
    h = h_ref[...]
    acc = None
    for b, y_ref in enumerate((ya_ref, yb_ref, yc_ref, yd_ref)):
        gate = jax.nn.sigmoid(_dot(h, wg_ref[b]) + bg_ref[b])
        term = gate * _dot(y_ref[...], wb_ref[b])
        acc = term if acc is None else acc + term
    o_ref[...] = acc.astype(o_ref.dtype)


def _merge(h2, ys, w_gate, b_gate, w_branch, layer, *, tm):
    n, d = h2.shape
    mix = w_branch.shape[2]
    tn = MERGE_TN
    y_spec = pl.BlockSpec((tm, mix), lambda i, j: (i, 0))
    return pl.pallas_call(
        _merge_kernel,
        grid=(n // tm, d // tn),
        in_specs=[
            pl.BlockSpec((tm, d), lambda i, j: (i, 0)),
            y_spec, y_spec, y_spec, y_spec,
            pl.BlockSpec((None, N_BRANCH, d, tn), lambda i, j: (layer, 0, 0, j)),
            pl.BlockSpec((N_BRANCH, 1, tn), lambda i, j: (0, 0, j)),
            pl.BlockSpec((None, N_BRANCH, mix, tn), lambda i, j: (layer, 0, 0, j)),
        ],
        out_specs=pl.BlockSpec((tm, tn), lambda i, j: (i, j)),
        out_shape=jax.ShapeDtypeStruct((n, d), BF16),
        compiler_params=_params("arbitrary", "arbitrary"),
        name="merge",
    )(h2, *ys, w_gate, b_gate.reshape(N_BRANCH, 1, d), w_branch)


def _outproj_kernel(x_ref, ga_b, ga_c, m_ref, w_ref, o_ref, *, tiles_per_batch, ctx_len):
    i = pl.program_id(0)
    tm, d = x_ref.shape
    m = m_ref[...]
    for c in range(0, d, OUT_TN):
        y = _dot(m, w_ref[:, c:c + OUT_TN])
        for rows, gate in _row_parts(i, tiles_per_batch, ctx_len, tm, ga_c, ga_b):
            o_ref[rows, c:c + OUT_TN] = x_ref[rows, c:c + OUT_TN] + gate[:, c:c + OUT_TN] * y[rows, :]


def _outproj(x2, mods, merged, w_out, layer, *, tm, tiles_per_batch, n_batch, ctx_len):
    n, d = x2.shape
    _, _, gate = mods
    bs, cs = _mod_specs(tiles_per_batch, n_batch, d)
    kern = functools.partial(_outproj_kernel, tiles_per_batch=tiles_per_batch, ctx_len=ctx_len)
    return pl.pallas_call(
        kern,
        grid=(n // tm,),
        in_specs=[
            pl.BlockSpec((tm, d), lambda i: (i, 0)),
            bs, cs,
            pl.BlockSpec((tm, d), lambda i: (i, 0)),
            pl.BlockSpec((None, d, d), lambda i: (layer, 0, 0), pipeline_mode=pl.Buffered(1)),
        ],
        out_specs=pl.BlockSpec((tm, d), lambda i: (i, 0)),
        out_shape=jax.ShapeDtypeStruct((n, d), F32),
        compiler_params=_params("arbitrary"),
        name="outproj",
    )(x2, gate, gate, merged, w_out)


def _final_kernel(x_ref, g_ref, o_ref):
    o_ref[...] = _rms(x_ref[...], g_ref[...])


def _final(x3, g, *, ctx_len, seq):
    nb, _, d = x3.shape
    off = ctx_len // ROW_BLOCK
    return pl.pallas_call(
        _final_kernel,
        grid=(nb, seq // ROW_BLOCK),
        in_specs=[
            pl.BlockSpec((None, ROW_BLOCK, d), lambda b, j: (b, j + off, 0)),
            pl.BlockSpec((1, d), lambda b, j: (0, 0)),
        ],
        out_specs=pl.BlockSpec((None, ROW_BLOCK, d), lambda b, j: (b, j, 0)),
        out_shape=jax.ShapeDtypeStruct((nb, seq, d), F32),
        compiler_params=_params("arbitrary", "arbitrary"),
        name="final",
    )(x3, g.reshape(1, d))


def _proj_layout(mix, n_kv):
    cdim = mix + 2 * SSD_GROUPS * SSD_STATE
    heads = mix // SSD_HEADDIM
    kvw = n_kv * HEAD_DIM
    sizes = dict(z=mix, xbc=cdim, dt=heads, gm=2 * mix, cv=2 * mix, q=mix, k=kvw, v=kvw)
    src, o = {}, 0
    for name in ("z", "xbc", "dt", "gm", "cv", "q", "k", "v"):
        src[name] = (o, sizes[name])
        o += sizes[name]
    order = ("xbc", "gm", "cv", "z", "q", "k", "v", "dt")
    dst, o = {}, 0
    for name in order:
        width = LANES if name == "dt" else sizes[name]
        assert o % width == 0
        dst[name] = o // width
        o += width
    total = -(-o // INPROJ_TN) * INPROJ_TN
    return src, order, dst, total


def _permute_w_in(w_in, src, order, total):
    cols = []
    for name in order:
        s, w = src[name]
        cols.append(w_in[:, :, s:s + w].astype(BF16))
    out = jnp.concatenate(cols, axis=2)
    return jnp.pad(out, ((0, 0), (0, 0), (0, total - out.shape[2])))


def kernel(x, c, ctx, c_ctx, w_ada, b_ada, ffn1_norm, ffn1_w13, ffn1_w2, mix_norm, w_in, w_gate, b_gate, w_branch, w_out, ssd_conv_w, ssd_conv_b, ssd_dt_bias, ssd_a_log, ssd_d, ssd_norm, gmlp_norm, gmlp_ws, gmlp_bs, conv_dw_w, conv_dw_b, conv_ln_g, conv_ln_b, attn_sink, ffn2_norm, ffn2_w13, ffn2_w2, final_norm):
    nb, seq, d = x.shape
    ctx_len = ctx.shape[1]
    depth = w_ada.shape[0]
    mix = w_branch.shape[2]
    n_kv = mix // HEAD_DIM // KV_RATIO
    r = ctx_len + seq
    assert ctx_len == ROW_BLOCK and seq % ROW_BLOCK == 0 and nb + 1 <= MOD_ROWS
    tm = TOKEN_TILE
    assert r % tm == 0
    tiles_per_batch = r // tm
    tile_kw = dict(tm=tm, tiles_per_batch=tiles_per_batch, n_batch=nb, ctx_len=ctx_len)

    src, order, dst, total = _proj_layout(mix, n_kv)
    cos_t, sin_t = _rope_tables(seq, ctx_len)

    ffn1_w13_b, ffn1_w2_b = ffn1_w13.astype(BF16), ffn1_w2.astype(BF16)
    ffn2_w13_b, ffn2_w2_b = ffn2_w13.astype(BF16), ffn2_w2.astype(BF16)
    w_in_b = _permute_w_in(w_in, src, order, total)
    w_gate_b, w_branch_b, w_out_b = w_gate.astype(BF16), w_branch.astype(BF16), w_out.astype(BF16)

    c_all = jnp.concatenate([c, c_ctx[None, :], jnp.zeros((MOD_ROWS - nb - 1, d), F32)], axis=0)
    mod_all = _ada(c_all, w_ada, b_ada).reshape(depth, MOD_ROWS, N_MOD, 1, d)

    xcur = jnp.concatenate([ctx, x], axis=1).reshape(nb * r, d)
    for l in range(depth):
        def mods(k, l=l):
            return tuple(mod_all[l, :, 3 * k + t] for t in range(3))

        xcur = _ffn(xcur, mods(0), ffn1_norm[l], ffn1_w13_b, ffn1_w2_b, l, **tile_kw)
        p2, h2 = _inproj(xcur, mods(1), mix_norm[l], w_in_b, l, **tile_kw)
        p3 = p2.reshape(nb, r, total)
        ya = _ssd(p3, ssd_conv_w[l], ssd_conv_b[l], ssd_dt_bias[l], ssd_a_log[l], ssd_d[l],
                  ssd_norm[l], dst, mix=mix)
        yb = _gmlp(p3, gmlp_norm[l], gmlp_ws[l], gmlp_bs[l], dst, mix=mix)
        yc = _conf(p3, conv_dw_w[l], conv_dw_b[l], conv_ln_g[l], conv_ln_b[l], dst, mix=mix)
        yd = _attn(p3, attn_sink[l], cos_t, sin_t, dst, mix=mix, ctx_len=ctx_len)
        ys = tuple(y.reshape(nb * r, mix) for y in (ya, yb, yc, yd))
        merged = _merge(h2, ys, w_gate_b, b_gate[l], w_branch_b, l, tm=tm)
        xcur = _outproj(xcur, mods(1), merged, w_out_b, l, **tile_kw)
        xcur = _ffn(xcur, mods(2), ffn2_norm[l], ffn2_w13_b, ffn2_w2_b, l, **tile_kw)
    return _final(xcur.reshape(nb, r, d), final_norm, ctx_len=ctx_len, seq=seq)
```

```python
import functools
import math

import jax
import jax.numpy as jnp
from jax import lax
from jax.experimental import pallas as pl
from jax.experimental.pallas import tpu as pltpu

F32 = jnp.float32
BF16 = jnp.bfloat16

GRID_W = 64
N_BRANCH = 4
N_MOD = 9
HALF_STEP = 0.5
SSD_HEADDIM = 64
SSD_GROUPS = 4
SSD_STATE = 128
SSD_CONV = 5
SSD_CHUNK = 128
GMLP_GROUPS = 8
GMLP_CHUNK = 128
CONV_KERNEL = 31
HEAD_DIM = 128
KV_RATIO = 4
WINDOW = 128
ROPE_BASE = 10000.0
EPS = 1e-6

LANES = 128
SUBLANES = 8
VMEM_LIMIT_BYTES = 56 * 1024 * 1024

TOKEN_TILE = 768
FFN_TF = 512
INPROJ_TN = 1792
MERGE_TN = 512
OUT_TN = 512
ADA_TN = 1024
ROW_BLOCK = 256
SSD_HALO = 8
CONF_HALO = 16
MOD_ROWS = 24


def _params(*sem):
    return pltpu.CompilerParams(dimension_semantics=sem, vmem_limit_bytes=VMEM_LIMIT_BYTES)


def _silu(x):
    return x * jax.nn.sigmoid(x)


def _dot(a, b):
    return jnp.dot(a, b, preferred_element_type=F32)


def _dot_nt(a, b):
    return lax.dot_general(a, b, (((1,), (1,)), ((), ())), preferred_element_type=F32)


def _rms(x, g):
    return x * lax.rsqrt(jnp.mean(x * x, axis=-1, keepdims=True) + EPS) * g


def _split3(v):
    hi = v.astype(BF16)
    r1 = v - hi.astype(F32)
    mid = r1.astype(BF16)
    lo = (r1 - mid.astype(F32)).astype(BF16)
    return hi, mid, lo


def _exact_dot_left(m01, v):
    hi, mid, lo = _split3(v)
    return _dot(m01, hi) + _dot(m01, mid) + _dot(m01, lo)


def _row_parts(tile, tiles_per_batch, ctx_len, tm, ctx_ref, batch_ref):
    first = tile % tiles_per_batch == 0
    top = jnp.where(first, ctx_ref[...], batch_ref[...])
    return ((slice(0, ctx_len), top), (slice(ctx_len, tm), batch_ref[...]))


def _modnorm_groups(h_ref, x_ref, g_ref, tile, tiles_per_batch, ctx_len, sh_b, sh_c, sc_b, sc_c):
    tm = x_ref.shape[0]
    first = tile % tiles_per_batch == 0
    for r0 in range(0, tm, ctx_len):
        rows = slice(r0, r0 + ctx_len)
        if r0 == 0:
            shift = jnp.where(first, sh_c[...], sh_b[...])
            scale = jnp.where(first, sc_c[...], sc_b[...])
        else:
            shift, scale = sh_b[...], sc_b[...]
        h_ref[rows, :] = (_rms(x_ref[rows, :], g_ref[...]) * (1.0 + scale) + shift).astype(h_ref.dtype)
        yield rows


def _ada_kernel(c_ref, w_ref, b_ref, o_ref):
    a = _silu(c_ref[...]).astype(BF16)
    o_ref[...] = _dot(a, w_ref[...].astype(BF16)) + b_ref[...]


def _ada(c_all, w_ada, b_ada):
    depth, d, nm = w_ada.shape
    tn = ADA_TN
    return pl.pallas_call(
        _ada_kernel,
        grid=(depth, nm // tn),
        in_specs=[
            pl.BlockSpec((MOD_ROWS, d), lambda l, n: (0, 0)),
            pl.BlockSpec((None, d, tn), lambda l, n: (l, 0, n)),
            pl.BlockSpec((None, 1, tn), lambda l, n: (l, 0, n)),
        ],
        out_specs=pl.BlockSpec((None, MOD_ROWS, tn), lambda l, n: (l, 0, n)),
        out_shape=jax.ShapeDtypeStruct((depth, MOD_ROWS, nm), F32),
        compiler_params=_params("arbitrary", "arbitrary"),
        name="ada",
    )(c_all, w_ada, b_ada.reshape(depth, 1, nm))


def _mod_specs(tiles_per_batch, n_batch, d):
    def batch_map(i, *_):
        return (i // tiles_per_batch, 0, 0)

    def ctx_map(i, *_):
        return (n_batch, 0, 0)

    return (pl.BlockSpec((None, 1, d), batch_map), pl.BlockSpec((None, 1, d), ctx_map))


def _ffn_kernel(x_ref, sh_b, sh_c, sc_b, sc_c, ga_b, ga_c, g_ref, w1_ref, w3_ref, w2_ref,
                o_ref, h_ref, *, tiles_per_batch, ctx_len, nf):
    i = pl.program_id(0)
    f = pl.program_id(1)
    tm = x_ref.shape[0]

    def swiglu(h):
        a = _dot(h, w1_ref[...])
        b = _dot(h, w3_ref[...])
        return _dot((_silu(a) * b).astype(BF16), w2_ref[...])

    @pl.when(f == 0)
    def _():
        for rows in _modnorm_groups(h_ref, x_ref, g_ref, i, tiles_per_batch, ctx_len,
                                    sh_b, sh_c, sc_b, sc_c):
            o_ref[rows, :] = swiglu(h_ref[rows, :])

    @pl.when(jnp.logical_and(f > 0, f < nf - 1))
    def _():
        o_ref[...] += swiglu(h_ref[...])

    @pl.when(f == nf - 1)
    def _():
        for rows, gate in _row_parts(i, tiles_per_batch, ctx_len, tm, ga_c, ga_b):
            acc = o_ref[rows, :] + swiglu(h_ref[rows, :])
            o_ref[rows, :] = x_ref[rows, :] + HALF_STEP * gate * acc


def _ffn(x2, mods, g, w13, w2, layer, *, tm, tiles_per_batch, n_batch, ctx_len):
    n, d = x2.shape
    fdim = w2.shape[1]
    tf = FFN_TF
    nf = fdim // tf
    assert nf >= 2
    shift, scale, gate = mods
    bs, cs = _mod_specs(tiles_per_batch, n_batch, d)
    kern = functools.partial(_ffn_kernel, tiles_per_batch=tiles_per_batch, ctx_len=ctx_len, nf=nf)
    return pl.pallas_call(
        kern,
        grid=(n // tm, nf),
        in_specs=[
            pl.BlockSpec((tm, d), lambda i, f: (i, 0)),
            bs, cs, bs, cs, bs, cs,
            pl.BlockSpec((1, d), lambda i, f: (0, 0)),
            pl.BlockSpec((None, d, tf), lambda i, f: (layer, 0, f)),
            pl.BlockSpec((None, d, tf), lambda i, f: (layer, 0, f + nf)),
            pl.BlockSpec((None, tf, d), lambda i, f: (layer, f, 0)),
        ],
        out_specs=pl.BlockSpec((tm, d), lambda i, f: (i, 0)),
        out_shape=jax.ShapeDtypeStruct((n, d), F32),
        scratch_shapes=[pltpu.VMEM((tm, d), BF16)],
        compiler_params=_params("arbitrary", "arbitrary"),
        name="ffn",
    )(x2, shift, shift, scale, scale, gate, gate, g.reshape(1, d), w13, w13, w2)


def _inproj_kernel(x_ref, sh_b, sh_c, sc_b, sc_c, g_ref, w_ref, o_ref, h_ref, *,
                   tiles_per_batch, ctx_len):
    i = pl.program_id(0)
    n = pl.program_id(1)

    @pl.when(n == 0)
    def _():
        for rows in _modnorm_groups(h_ref, x_ref, g_ref, i, tiles_per_batch, ctx_len,
                                    sh_b, sh_c, sc_b, sc_c):
            o_ref[rows, :] = _dot(h_ref[rows, :], w_ref[...])

    @pl.when(n > 0)
    def _():
        o_ref[...] = _dot(h_ref[...], w_ref[...])


def _inproj(x2, mods, g, w, layer, *, tm, tiles_per_batch, n_batch, ctx_len):
    n, d = x2.shape
    width = w.shape[2]
    tn = INPROJ_TN
    shift, scale, _ = mods
    bs, cs = _mod_specs(tiles_per_batch, n_batch, d)
    kern = functools.partial(_inproj_kernel, tiles_per_batch=tiles_per_batch, ctx_len=ctx_len)
    return pl.pallas_call(
        kern,
        grid=(n // tm, width // tn),
        in_specs=[
            pl.BlockSpec((tm, d), lambda i, j: (i, 0)),
            bs, cs, bs, cs,
            pl.BlockSpec((1, d), lambda i, j: (0, 0)),
            pl.BlockSpec((None, d, tn), lambda i, j: (layer, 0, j)),
        ],
        out_specs=[pl.BlockSpec((tm, tn), lambda i, j: (i, j)),
                   pl.BlockSpec((tm, d), lambda i, j: (i, 0))],
        out_shape=[jax.ShapeDtypeStruct((n, width), F32), jax.ShapeDtypeStruct((n, d), BF16)],
        compiler_params=_params("arbitrary", "arbitrary"),
        name="inproj",
    )(x2, shift, shift, scale, scale, g.reshape(1, d), w)


def _ssd_block_of(d, j, nblk):
    return jnp.where(d == 0, j, jnp.where(j == 0, 0, nblk - j))


def _ssd_kernel(prev_ref, cur_ref, next_ref, z_ref, dt_ref, cw_ref, cb_ref, dtb_ref, alog_ref,
                dsk_ref, ng_ref, e3_ref, o_ref, xw_ref, x_ref, b_ref, c_ref, bt_ref, yf_ref, st_ref,
                *, nblk, mix):
    d = pl.program_id(1)
    j = pl.program_id(2)
    blk = _ssd_block_of(d, j, nblk)
    q = SSD_CHUNK
    gw = SSD_STATE
    bcw = SSD_GROUPS * gw
    hpg = mix // SSD_HEADDIM // SSD_GROUPS
    nch = ROW_BLOCK // q
    row0 = pl.multiple_of(blk * ROW_BLOCK, ROW_BLOCK)
    pad = (SSD_CONV - 1) // 2

    @pl.when(j == 0)
    def _():
        st_ref[...] = jnp.zeros_like(st_ref)

    ii = lax.broadcasted_iota(jnp.int32, (q, q), 0)
    jj = lax.broadcasted_iota(jnp.int32, (q, q), 1)
    lane = lax.broadcasted_iota(jnp.int32, (q, LANES), 1)
    lo_half = lane < SSD_HEADDIM

    def conv_chunk(ci):
        base = ci * q
        acc = jnp.broadcast_to(cb_ref[...], (q, cb_ref.shape[1]))
        for k in range(SSD_CONV):
            acc = acc + cw_ref[k:k + 1, :] * xw_ref[pl.ds(SSD_HALO - pad + k + base, q), :]
        hs = _silu(acc)
        grow = pl.multiple_of(row0 + base, q)
        x_ref[pl.ds(grow, q), :] = hs[:, 0:mix]
        b_ref[pl.ds(grow, q), :] = hs[:, mix:mix + bcw].astype(BF16)
        c_ref[pl.ds(grow, q), :] = hs[:, mix + bcw:mix + 2 * bcw].astype(BF16)
        for g in range(SSD_GROUPS):
            bt_ref[blk * nch + ci, g * gw:(g + 1) * gw, :] = hs[:, mix + g * gw:mix + (g + 1) * gw].T

    def scan_chunk(chunk, backward):
        tri = (jj >= ii) if backward else (jj <= ii)
        tri_b = jnp.where(tri, 1.0, 0.0).astype(BF16)
        a_neg = -jnp.exp(alog_ref[...])
        e3 = e3_ref[...]
        r0 = chunk * q
        grow = pl.multiple_of(row0 + r0, q)
        x = x_ref[pl.ds(grow, q), :]
        dt_raw = dt_ref[r0:r0 + q, :] + dtb_ref[...]
        dt = jnp.maximum(dt_raw, 0.0) + jnp.log1p(jnp.exp(-jnp.abs(dt_raw)))
        a = dt * a_neg
        cum = _exact_dot_left(tri_b, a)
        total = jnp.sum(a, axis=0, keepdims=True)
        cum_t = cum.T
        ecum = jnp.exp(cum)
        dt_t = dt.T
        s_t = (dt * jnp.exp(total - cum)).T
        v = jnp.concatenate([ecum, jnp.broadcast_to(jnp.exp(total), (SUBLANES, LANES))], axis=0)
        ex = _dot(jnp.concatenate(_split3(v), axis=1), e3)
        ec_x, cd_x = ex[0:q], ex[q:q + 1]
        ys = []
        for g in range(SSD_GROUPS):
            bg_b = b_ref[pl.ds(grow, q), g * gw:(g + 1) * gw]
            cg_b = c_ref[pl.ds(grow, q), g * gw:(g + 1) * gw]
            btg = bt_ref[blk * nch + chunk, g * gw:(g + 1) * gw, :]
            cb = _dot_nt(cg_b, bg_b)
            c0 = g * hpg * SSD_HEADDIM
            cw = hpg * SSD_HEADDIM
            y_off = _dot(cg_b, st_ref[:, c0:c0 + cw].astype(BF16)) * ec_x[:, c0:c0 + cw]
            y_pairs = []
            for pr in range(hpg // 2):
                l0 = c0 + pr * LANES
                xp = x[:, l0:l0 + LANES]
                ms, bts, xms = [], [], []
                for half in range(2):
                    h = g * hpg + pr * 2 + half
                    seg = cum[:, h:h + 1] - cum_t[h:h + 1, :]
                    dec = jnp.exp(jnp.where(tri, seg, -jnp.inf))
                    keep = lo_half if half == 0 else jnp.logical_not(lo_half)
                    ms.append((cb * dec * dt_t[h:h + 1, :]).astype(BF16))
                    bts.append((btg * s_t[h:h + 1, :]).astype(BF16))
                    xms.append(jnp.where(keep, xp, 0.0).astype(BF16))
                xm = jnp.concatenate(xms, axis=0)
                y_pairs.append(_dot(jnp.concatenate(ms, axis=1), xm))
                st_new = _dot(jnp.concatenate(bts, axis=1), xm)
                st_ref[:, l0:l0 + LANES] = st_ref[:, l0:l0 + LANES] * cd_x[:, l0:l0 + LANES] + st_new
            ys.append(jnp.concatenate(y_pairs, axis=1) + y_off)
        y = jnp.concatenate(ys, axis=1) + x * dsk_ref[...]
        if backward:
            yt = (yf_ref[pl.ds(grow, q), :] + y) * _silu(z_ref[r0:r0 + q, :])
            o_ref[r0:r0 + q, :] = _rms(yt, ng_ref[...]).astype(o_ref.dtype)
        else:
            yf_ref[pl.ds(grow, q), :] = y

    @pl.when(d == 0)
    def _():
        has_prev = jnp.logical_and(blk != 0, blk != 1)
        has_next = jnp.logical_and(blk != 0, blk != nblk - 1)
        xw_ref[0:SSD_HALO, :] = jnp.where(has_prev, prev_ref[...], 0.0)
        xw_ref[SSD_HALO:SSD_HALO + ROW_BLOCK, :] = cur_ref[...]
        xw_ref[SSD_HALO + ROW_BLOCK:, :] = jnp.where(has_next, next_ref[...], 0.0)
        for ci in range(nch):
            conv_chunk(ci)
            scan_chunk(ci, False)

    @pl.when(d == 1)
    def _():
        for ci in range(nch):
            scan_chunk(nch - 1 - ci, True)


def _ssd(p3, conv_w, conv_b, dt_bias, a_log, d_skip, norm_g, lay, *, mix):
    nb, r, _ = p3.shape
    nblk = r // ROW_BLOCK
    cdim = conv_w.shape[1]
    heads = mix // SSD_HEADDIM
    hb = ROW_BLOCK // SSD_HALO
    bcw = SSD_GROUPS * SSD_STATE

    def blk_map(b, d, j):
        return _ssd_block_of(d, j, nblk)

    def conv_blk(b, d, j):
        return jnp.where(d == 0, j, nblk - 1)

    pad_h = LANES - heads
    dtb = jnp.pad(dt_bias, ((0, 0), (0, pad_h))).reshape(2, 1, LANES)
    alog = jnp.pad(a_log, ((0, 0), (0, pad_h))).reshape(2, 1, LANES)
    dsk = jnp.repeat(d_skip, SSD_HEADDIM, axis=1).reshape(2, 1, mix)
    e01 = (jnp.arange(LANES)[:, None] == (jnp.arange(mix)[None, :] // SSD_HEADDIM)).astype(BF16)
    e3 = jnp.concatenate([e01, e01, e01], axis=0)
    kern = functools.partial(_ssd_kernel, nblk=nblk, mix=mix)
    xbc_c, z_c, dt_c = lay["xbc"], lay["z"], lay["dt"]
    return pl.pallas_call(
        kern,
        grid=(nb, 2, nblk),
        in_specs=[
            pl.BlockSpec((None, SSD_HALO, cdim),
                         lambda b, d, j: (b, jnp.maximum(conv_blk(b, d, j) * hb - 1, 0), xbc_c)),
            pl.BlockSpec((None, ROW_BLOCK, cdim), lambda b, d, j: (b, conv_blk(b, d, j), xbc_c)),
            pl.BlockSpec((None, SSD_HALO, cdim),
                         lambda b, d, j: (b, jnp.minimum((conv_blk(b, d, j) + 1) * hb, nblk * hb - 1), xbc_c)),
            pl.BlockSpec((None, ROW_BLOCK, mix),
                         lambda b, d, j: (b, jnp.where(d == 0, 0, blk_map(b, d, j)), z_c)),
            pl.BlockSpec((None, ROW_BLOCK, LANES), lambda b, d, j: (b, blk_map(b, d, j), dt_c)),
            pl.BlockSpec((SSD_CONV, cdim), lambda b, d, j: (0, 0)),
            pl.BlockSpec((1, cdim), lambda b, d, j: (0, 0)),
            pl.BlockSpec((None, 1, LANES), lambda b, d, j: (d, 0, 0)),
            pl.BlockSpec((None, 1, LANES), lambda b, d, j: (d, 0, 0)),
            pl.BlockSpec((None, 1, mix), lambda b, d, j: (d, 0, 0)),
            pl.BlockSpec((1, mix), lambda b, d, j: (0, 0)),
            pl.BlockSpec((3 * LANES, mix), lambda b, d, j: (0, 0)),
        ],
        out_specs=pl.BlockSpec((None, ROW_BLOCK, mix),
                               lambda b, d, j: (b, jnp.where(d == 0, 0, blk_map(b, d, j)), 0)),
        out_shape=jax.ShapeDtypeStruct((nb, r, mix), BF16),
        scratch_shapes=[
            pltpu.VMEM((ROW_BLOCK + 2 * SSD_HALO, cdim), F32),
            pltpu.VMEM((r, mix), F32),
            pltpu.VMEM((r, bcw), BF16),
            pltpu.VMEM((r, bcw), BF16),
            pltpu.VMEM((r // SSD_CHUNK, bcw, SSD_CHUNK), F32),
            pltpu.VMEM((r, mix), F32),
            pltpu.VMEM((SSD_STATE, mix), F32),
        ],
        compiler_params=_params("arbitrary", "arbitrary", "arbitrary"),
        name="ssd",
    )(p3, p3, p3, p3, p3, conv_w, conv_b.reshape(1, cdim), dtb, alog, dsk,
      norm_g.reshape(1, mix), e3)


def _gmlp_kernel(p_ref, g_ref, ws_ref, bias_ref, o_ref, *, mix):
    p = p_ref[...]
    ge = 0.5 * p * (1.0 + jnp.tanh(math.sqrt(2.0 / math.pi) * (p + 0.044715 * (p * p * p))))
    u = ge[:, :mix]
    v = _rms(ge[:, mix:], g_ref[...]).astype(BF16)
    gw = mix // GMLP_GROUPS
    for c in range(ROW_BLOCK // GMLP_CHUNK):
        r0 = c * GMLP_CHUNK
        outs = []
        for g in range(GMLP_GROUPS):
            outs.append(_dot(ws_ref[g], v[r0:r0 + GMLP_CHUNK, g * gw:(g + 1) * gw]))
        mixed = jnp.concatenate(outs, axis=1) + bias_ref[...]
        o_ref[r0:r0 + GMLP_CHUNK, :] = (u[r0:r0 + GMLP_CHUNK, :] * mixed).astype(o_ref.dtype)


def _gmlp(p3, norm_g, ws, bs, lay, *, mix):
    nb, r, _ = p3.shape
    nblk = r // ROW_BLOCK
    bias_full = jnp.repeat(bs.T, mix // GMLP_GROUPS, axis=1)
    gm_c = lay["gm"]
    return pl.pallas_call(
        functools.partial(_gmlp_kernel, mix=mix),
        grid=(nb, nblk),
        in_specs=[
            pl.BlockSpec((None, ROW_BLOCK, 2 * mix), lambda b, j: (b, j, gm_c)),
            pl.BlockSpec((1, mix), lambda b, j: (0, 0)),
            pl.BlockSpec((GMLP_GROUPS, GMLP_CHUNK, GMLP_CHUNK), lambda b, j: (0, 0, 0)),
            pl.BlockSpec((GMLP_CHUNK, mix), lambda b, j: (0, 0)),
        ],
        out_specs=pl.BlockSpec((None, ROW_BLOCK, mix), lambda b, j: (b, j, 0)),
        out_shape=jax.ShapeDtypeStruct((nb, r, mix), BF16),
        compiler_params=_params("arbitrary", "arbitrary"),
        name="gmlp",
    )(p3, norm_g.reshape(1, mix), ws.astype(BF16), bias_full)


def _conf_kernel(prev_ref, cur_ref, next_ref, w_ref, b_ref, lg_ref, lb_ref, o_ref, gw_ref, ph_ref,
                 *, nblk, mix):
    j = pl.program_id(1)

    def glu(v):
        return v[:, :mix] * jax.nn.sigmoid(v[:, mix:])

    has_prev = jnp.logical_and(j != 0, j != 1)
    has_next = jnp.logical_and(j != 0, j != nblk - 1)
    gw_ref[0:CONF_HALO, :] = jnp.where(has_prev, glu(prev_ref[...]), 0.0)
    gw_ref[CONF_HALO:CONF_HALO + ROW_BLOCK, :] = glu(cur_ref[...])
    gw_ref[CONF_HALO + ROW_BLOCK:, :] = jnp.where(has_next, glu(next_ref[...]), 0.0)
    span = ph_ref.shape[1]
    for r in range(1, SUBLANES):
        ph_ref[r - 1] = gw_ref[pl.ds(r, span), :]
    pad = (CONV_KERNEL - 1) // 2
    acc = jnp.broadcast_to(b_ref[...], (ROW_BLOCK, mix))
    for k in range(CONV_KERNEL):
        a, r = divmod(CONF_HALO - pad + k, SUBLANES)
        if r == 0:
            win = gw_ref[pl.ds(a * SUBLANES, ROW_BLOCK), :]
        else:
            win = ph_ref[r - 1, pl.ds(a * SUBLANES, ROW_BLOCK), :]
        acc = acc + w_ref[k:k + 1, :] * win
    mu = jnp.mean(acc, axis=-1, keepdims=True)
    cen = acc - mu
    var = jnp.mean(cen * cen, axis=-1, keepdims=True)
    y = cen * lax.rsqrt(var + EPS) * lg_ref[...] + lb_ref[...]
    o_ref[...] = _silu(y).astype(o_ref.dtype)


def _conf(p3, dw_w, dw_b, ln_g, ln_b, lay, *, mix):
    nb, r, _ = p3.shape
    nblk = r // ROW_BLOCK
    hb = ROW_BLOCK // CONF_HALO
    cv_c = lay["cv"]
    win_rows = ROW_BLOCK + 2 * CONF_HALO
    return pl.pallas_call(
        functools.partial(_conf_kernel, nblk=nblk, mix=mix),
        grid=(nb, nblk),
        in_specs=[
            pl.BlockSpec((None, CONF_HALO, 2 * mix), lambda b, j: (b, jnp.maximum(j * hb - 1, 0), cv_c)),
            pl.BlockSpec((None, ROW_BLOCK, 2 * mix), lambda b, j: (b, j, cv_c)),
            pl.BlockSpec((None, CONF_HALO, 2 * mix),
                         lambda b, j: (b, jnp.minimum((j + 1) * hb, nblk * hb - 1), cv_c)),
            pl.BlockSpec((CONV_KERNEL, mix), lambda b, j: (0, 0)),
            pl.BlockSpec((1, mix), lambda b, j: (0, 0)),
            pl.BlockSpec((1, mix), lambda b, j: (0, 0)),
            pl.BlockSpec((1, mix), lambda b, j: (0, 0)),
        ],
        out_specs=pl.BlockSpec((None, ROW_BLOCK, mix), lambda b, j: (b, j, 0)),
        out_shape=jax.ShapeDtypeStruct((nb, r, mix), BF16),
        scratch_shapes=[pltpu.VMEM((win_rows, mix), F32),
                        pltpu.VMEM((SUBLANES - 1, win_rows - SUBLANES, mix), F32)],
        compiler_params=_params("arbitrary", "arbitrary"),
        name="conf",
    )(p3, p3, p3, dw_w, dw_b.reshape(1, mix), ln_g.reshape(1, mix), ln_b.reshape(1, mix))


def _rope(x, cos, sin_signed):
    lane = lax.broadcasted_iota(jnp.int32, x.shape, 1)
    first = (lane & (HEAD_DIM // 4)) == 0
    partner = jnp.where(first, pltpu.roll(x, HEAD_DIM - HEAD_DIM // 4, 1), pltpu.roll(x, HEAD_DIM // 4, 1))
    return x * cos + partner * sin_signed


def _lane_blocks(x):
    return [x[:, c:c + LANES] for c in range(0, x.shape[1], LANES)]


def _attn_kernel(sink_ref, q_ref, k_ref, v_ref, cos_ref, sin_ref, o_ref, kr_ref, vb_ref, *,
                 ctx_len, n_kv, rows):
    j = pl.program_id(1)
    blk = WINDOW
    scale = HEAD_DIM ** -0.5

    @pl.when(j == 0)
    def _():
        for g in range(n_kv):
            kr_ref[:, g * HEAD_DIM:(g + 1) * HEAD_DIM] = _rope(
                k_ref[:, g * HEAD_DIM:(g + 1) * HEAD_DIM], cos_ref[...], sin_ref[...]).astype(BF16)
        vb_ref[...] = v_ref[...].astype(BF16)

    hrow = lax.broadcasted_iota(jnp.int32, (KV_RATIO * blk, 1), 0) // blk
    for sb in range(q_ref.shape[0] // blk):
        qrows = slice(sb * blk, (sb + 1) * blk)
        jb = j * (q_ref.shape[0] // blk) + sb
        q0 = pl.multiple_of(jb * blk, blk)
        cos_q = cos_ref[pl.ds(q0, blk), :]
        sin_q = sin_ref[pl.ds(q0, blk), :]
        band0 = pl.multiple_of(jnp.clip((jb - 1) * blk, 0, rows - 3 * blk), blk)
        qpos = q0 + lax.broadcasted_iota(jnp.int32, (blk, 3 * blk), 0)
        kpos = band0 + lax.broadcasted_iota(jnp.int32, (blk, 3 * blk), 1)
        ok = jnp.logical_and(jnp.logical_and(qpos >= ctx_len, kpos >= ctx_len),
                             jnp.abs(kpos - qpos) <= WINDOW)
        bias = jnp.where(ok, 0.0, -jnp.inf)
        bias = jnp.concatenate([bias] * KV_RATIO, axis=0)

        for g in range(n_kv):
            gc = slice(g * HEAD_DIM, (g + 1) * HEAD_DIM)
            k_band = kr_ref[pl.ds(band0, 3 * blk), gc]
            v_band = vb_ref[pl.ds(band0, 3 * blk), gc]
            k_ctx = kr_ref[0:ctx_len, gc]
            v_ctx = vb_ref[0:ctx_len, gc]
            qs = []
            sink = jnp.zeros((KV_RATIO * blk, 1), F32)
            for r in range(KV_RATIO):
                h = g * KV_RATIO + r
                qs.append(_rope(q_ref[qrows, h * HEAD_DIM:(h + 1) * HEAD_DIM], cos_q, sin_q).astype(BF16))
                sink = jnp.where(hrow == r, sink_ref[h], sink)
            qg = jnp.concatenate(qs, axis=0)
            s_band = _dot_nt(qg, k_band) * scale + bias
            s_ctx = _dot_nt(qg, k_ctx) * scale
            mx = functools.reduce(jnp.maximum, _lane_blocks(s_band) + _lane_blocks(s_ctx))
            m = jnp.maximum(jnp.max(mx, axis=-1, keepdims=True), sink)
            p_band = jnp.exp(s_band - m)
            p_ctx = jnp.exp(s_ctx - m)
            psum = functools.reduce(lambda u, w: u + w, _lane_blocks(p_band) + _lane_blocks(p_ctx))
            den = jnp.sum(psum, axis=-1, keepdims=True) + jnp.exp(sink - m)
            o = (_dot(p_band.astype(BF16), v_band) + _dot(p_ctx.astype(BF16), v_ctx)) / den
            for r in range(KV_RATIO):
                h = g * KV_RATIO + r
                o_ref[qrows, h * HEAD_DIM:(h + 1) * HEAD_DIM] = o[r * blk:(r + 1) * blk, :].astype(o_ref.dtype)


def _attn(p3, sink, cos_t, sin_t, lay, *, mix, ctx_len):
    nb, r, _ = p3.shape
    n_heads = mix // HEAD_DIM
    n_kv = n_heads // KV_RATIO
    kvw = n_kv * HEAD_DIM
    q_c, k_c, v_c = lay["q"], lay["k"], lay["v"]
    kern = functools.partial(_attn_kernel, ctx_len=ctx_len, n_kv=n_kv, rows=r)
    return pl.pallas_call(
        kern,
        grid=(nb, r // ROW_BLOCK),
        in_specs=[
            pl.BlockSpec(memory_space=pltpu.SMEM),
            pl.BlockSpec((None, ROW_BLOCK, mix), lambda b, j: (b, j, q_c)),
            pl.BlockSpec((None, r, kvw), lambda b, j: (b, 0, k_c)),
            pl.BlockSpec((None, r, kvw), lambda b, j: (b, 0, v_c)),
            pl.BlockSpec((r, HEAD_DIM), lambda b, j: (0, 0)),
            pl.BlockSpec((r, HEAD_DIM), lambda b, j: (0, 0)),
        ],
        out_specs=pl.BlockSpec((None, ROW_BLOCK, mix), lambda b, j: (b, j, 0)),
        out_shape=jax.ShapeDtypeStruct((nb, r, mix), BF16),
        scratch_shapes=[pltpu.VMEM((r, kvw), BF16), pltpu.VMEM((r, kvw), BF16)],
        compiler_params=_params("arbitrary", "arbitrary"),
        name="attn",
    )(sink, p3, p3, p3, cos_t, sin_t)


def _rope_tables(seq, ctx_len):
    quarter = HEAD_DIM // 4
    t = jnp.arange(seq, dtype=jnp.int32)
    rows = (t // GRID_W).astype(F32)
    cols = (t % GRID_W).astype(F32)
    freqs = ROPE_BASE ** (-jnp.arange(quarter, dtype=F32) / quarter)
    ang_r = rows[:, None] * freqs
    ang_c = cols[:, None] * freqs
    cos = jnp.concatenate([jnp.cos(ang_r)] * 2 + [jnp.cos(ang_c)] * 2, axis=1)
    sin = jnp.concatenate([-jnp.sin(ang_r), jnp.sin(ang_r), -jnp.sin(ang_c), jnp.sin(ang_c)], axis=1)
    cos = jnp.concatenate([jnp.ones((ctx_len, HEAD_DIM), F32), cos], axis=0)
    sin = jnp.concatenate([jnp.zeros((ctx_len, HEAD_DIM), F32), sin], axis=0)
    return cos, sin


def _merge_kernel(h_ref, ya_ref, yb_ref, yc_ref, yd_ref, wg_ref, bg_ref, wb_ref, o_ref):
    h = h_ref[...]
    acc = None
    for b, y_ref in enumerate((ya_ref, yb_ref, yc_ref, yd_ref)):
        gate = jax.nn.sigmoid(_dot(h, wg_ref[b]) + bg_ref[b])
        term = gate * _dot(y_ref[...], wb_ref[b])
        acc = term if acc is None else acc + term
    o_ref[...] = acc.astype(o_ref.dtype)


def _merge(h2, ys, w_gate, b_gate, w_branch, layer, *, tm):
    n, d = h2.shape
    mix = w_branch.shape[2]
    tn = MERGE_TN
    y_spec = pl.BlockSpec((tm, mix), lambda i, j: (i, 0))
    return pl.pallas_call(
        _merge_kernel,
        grid=(n // tm, d // tn),
        in_specs=[
            pl.BlockSpec((tm, d), lambda i, j: (i, 0)),
            y_spec, y_spec, y_spec, y_spec,
            pl.BlockSpec((None, N_BRANCH, d, tn), lambda i, j: (layer, 0, 0, j)),
            pl.BlockSpec((N_BRANCH, 1, tn), lambda i, j: (0, 0, j)),
            pl.BlockSpec((None, N_BRANCH, mix, tn), lambda i, j: (layer, 0, 0, j)),
        ],
        out_specs=pl.BlockSpec((tm, tn), lambda i, j: (i, j)),
        out_shape=jax.ShapeDtypeStruct((n, d), BF16),
        compiler_params=_params("arbitrary", "arbitrary"),
        name="merge",
    )(h2, *ys, w_gate, b_gate.reshape(N_BRANCH, 1, d), w_branch)


def _outproj_kernel(x_ref, ga_b, ga_c, m_ref, w_ref, o_ref, *, tiles_per_batch, ctx_len):
    i = pl.program_id(0)
    tm, d = x_ref.shape
    m = m_ref[...]
    for c in range(0, d, OUT_TN):
        y = _dot(m, w_ref[:, c:c + OUT_TN])
        for rows, gate in _row_parts(i, tiles_per_batch, ctx_len, tm, ga_c, ga_b):
            o_ref[rows, c:c + OUT_TN] = x_ref[rows, c:c + OUT_TN] + gate[:, c:c + OUT_TN] * y[rows, :]


def _outproj(x2, mods, merged, w_out, layer, *, tm, tiles_per_batch, n_batch, ctx_len):
    n, d = x2.shape
    _, _, gate = mods
    bs, cs = _mod_specs(tiles_per_batch, n_batch, d)
    kern = functools.partial(_outproj_kernel, tiles_per_batch=tiles_per_batch, ctx_len=ctx_len)
    return pl.pallas_call(
        kern,
        grid=(n // tm,),
        in_specs=[
            pl.BlockSpec((tm, d), lambda i: (i, 0)),
            bs, cs,
            pl.BlockSpec((tm, d), lambda i: (i, 0)),
            pl.BlockSpec((None, d, d), lambda i: (layer, 0, 0), pipeline_mode=pl.Buffered(1)),
        ],
        out_specs=pl.BlockSpec((tm, d), lambda i: (i, 0)),
        out_shape=jax.ShapeDtypeStruct((n, d), F32),
        compiler_params=_params("arbitrary"),
        name="outproj",
    )(x2, gate, gate, merged, w_out)


def _final_kernel(x_ref, g_ref, o_ref):
    o_ref[...] = _rms(x_ref[...], g_ref[...])


def _final(x3, g, *, ctx_len, seq):
    nb, _, d = x3.shape
    off = ctx_len // ROW_BLOCK
    return pl.pallas_call(
        _final_kernel,
        grid=(nb, seq // ROW_BLOCK),
        in_specs=[
            pl.BlockSpec((None, ROW_BLOCK, d), lambda b, j: (b, j + off, 0)),
            pl.BlockSpec((1, d), lambda b, j: (0, 0)),
        ],
        out_specs=pl.BlockSpec((None, ROW_BLOCK, d), lambda b, j: (b, j, 0)),
        out_shape=jax.ShapeDtypeStruct((nb, seq, d), F32),
        compiler_params=_params("arbitrary", "arbitrary"),
        name="final",
    )(x3, g.reshape(1, d))


def _proj_layout(mix, n_kv):
    cdim = mix + 2 * SSD_GROUPS * SSD_STATE
    heads = mix // SSD_HEADDIM
    kvw = n_kv * HEAD_DIM
    sizes = dict(z=mix, xbc=cdim, dt=heads, gm=2 * mix, cv=2 * mix, q=mix, k=kvw, v=kvw)
    src, o = {}, 0
    for name in ("z", "xbc", "dt", "gm", "cv", "q", "k", "v"):
        src[name] = (o, sizes[name])
        o += sizes[name]
    order = ("xbc", "gm", "cv", "z", "q", "k", "v", "dt")
    dst, o = {}, 0
    for name in order:
        width = LANES if name == "dt" else sizes[name]
        assert o % width == 0
        dst[name] = o // width
        o += width
    total = -(-o // INPROJ_TN) * INPROJ_TN
    return src, order, dst, total


def _permute_w_in(w_in, src, order, total):
    cols = []
    for name in order:
        s, w = src[name]
        cols.append(w_in[:, :, s:s + w].astype(BF16))
    out = jnp.concatenate(cols, axis=2)
    return jnp.pad(out, ((0, 0), (0, 0), (0, total - out.shape[2])))


def kernel(x, c, ctx, c_ctx, w_ada, b_ada, ffn1_norm, ffn1_w13, ffn1_w2, mix_norm, w_in, w_gate, b_gate, w_branch, w_out, ssd_conv_w, ssd_conv_b, ssd_dt_bias, ssd_a_log, ssd_d, ssd_norm, gmlp_norm, gmlp_ws, gmlp_bs, conv_dw_w, conv_dw_b, conv_ln_g, conv_ln_b, attn_sink, ffn2_norm, ffn2_w13, ffn2_w2, final_norm):
    nb, seq, d = x.shape
    ctx_len = ctx.shape[1]
    depth = w_ada.shape[0]
    mix = w_branch.shape[2]
    n_kv = mix // HEAD_DIM // KV_RATIO
    r = ctx_len + seq
    assert ctx_len == ROW_BLOCK and seq % ROW_BLOCK == 0 and nb + 1 <= MOD_ROWS
    tm = TOKEN_TILE
    assert r % tm == 0
    tiles_per_batch = r // tm
    tile_kw = dict(tm=tm, tiles_per_batch=tiles_per_batch, n_batch=nb, ctx_len=ctx_len)

    src, order, dst, total = _proj_layout(mix, n_kv)
    cos_t, sin_t = _rope_tables(seq, ctx_len)

    ffn1_w13_b, ffn1_w2_b = ffn1_w13.astype(BF16), ffn1_w2.astype(BF16)
    ffn2_w13_b, ffn2_w2_b = ffn2_w13.astype(BF16), ffn2_w2.astype(BF16)
    w_in_b = _permute_w_in(w_in, src, order, total)
    w_gate_b, w_branch_b, w_out_b = w_gate.astype(BF16), w_branch.astype(BF16), w_out.astype(BF16)

    c_all = jnp.concatenate([c, c_ctx[None, :], jnp.zeros((MOD_ROWS - nb - 1, d), F32)], axis=0)
    mod_all = _ada(c_all, w_ada, b_ada).reshape(depth, MOD_ROWS, N_MOD, 1, d)

    xcur = jnp.concatenate([ctx, x], axis=1).reshape(nb * r, d)
    for l in range(depth):
        def mods(k, l=l):
            return tuple(mod_all[l, :, 3 * k + t] for t in range(3))

        xcur = _ffn(xcur, mods(0), ffn1_norm[l], ffn1_w13_b, ffn1_w2_b, l, **tile_kw)
        p2, h2 = _inproj(xcur, mods(1), mix_norm[l], w_in_b, l, **tile_kw)
        p3 = p2.reshape(nb, r, total)
        ya = _ssd(p3, ssd_conv_w[l], ssd_conv_b[l], ssd_dt_bias[l], ssd_a_log[l], ssd_d[l],
                  ssd_norm[l], dst, mix=mix)
        yb = _gmlp(p3, gmlp_norm[l], gmlp_ws[l], gmlp_bs[l], dst, mix=mix)
        yc = _conf(p3, conv_dw_w[l], conv_dw_b[l], conv_ln_g[l], conv_ln_b[l], dst, mix=mix)
        yd = _attn(p3, attn_sink[l], cos_t, sin_t, dst, mix=mix, ctx_len=ctx_len)
        ys = tuple(y.reshape(nb * r, mix) for y in (ya, yb, yc, yd))
        merged = _merge(h2, ys, w_gate_b, b_gate[l], w_branch_b, l, tm=tm)
        xcur = _outproj(xcur, mods(1), merged, w_out_b, l, **tile_kw)
        xcur = _ffn(xcur, mods(2), ffn2_norm[l], ffn2_w13_b, ffn2_w2_b, l, **tile_kw)
    return _final(xcur.reshape(nb, r, d), final_norm, ctx_len=ctx_len, seq=seq)
```
